```python
import math
import jax, jax.numpy as jnp
from jax import lax
import numpy as np

D_MODEL = 1024
BATCH = 4
SEQ = 8192
DEPTH = 1
DEC_BATCH = 128
DEC_SEQ = 4
PAST_LEN = 8192
PAGE_SIZE = 128

N_META = 16
POOL_WIDTH = D_MODEL // 2
POOL_WINDOWS = (2, 4, 8, 16)
POOL_GROUPS = len(POOL_WINDOWS)
POOL_GROUP_WIDTH = POOL_WIDTH // POOL_GROUPS
POOL_BUF = max(POOL_WINDOWS) - 1
N_HEADS = 8
HEAD_DIM = 64
ATTN_WIDTH = N_HEADS * HEAD_DIM
N_IDX_HEADS = 4
IDX_DIM = 64
TOPK_MAX = 256
N_BUCKETS = 32
MAX_DISTANCE = 128
Q_BLOCK = 128
ALPHA = (2 * DEPTH) ** 0.25
BETA = (8 * DEPTH) ** -0.25
LN_EPS = 1e-5
SPLITS = (POOL_WIDTH, POOL_WIDTH,
          ATTN_WIDTH, ATTN_WIDTH, ATTN_WIDTH, ATTN_WIDTH,
          N_IDX_HEADS * IDX_DIM, IDX_DIM, N_IDX_HEADS,
          D_MODEL, D_MODEL)
IN_WIDTH = sum(SPLITS)

kernel_name = 'hybrid_pool_dsa_decode_step'


def _project(x, w_in):
    h = jnp.einsum('btd,dc->btc', x, w_in)
    parts, o = [], 0
    for n in SPLITS:
        parts.append(h[..., o:o + n])
        o += n
    return parts


def _layer_norm(z, g, b):
    z32 = z.astype(jnp.float32)
    mu = jnp.mean(z32, axis=-1, keepdims=True)
    var = jnp.mean(jnp.square(z32 - mu), axis=-1, keepdims=True)
    return ((z32 - mu) * lax.rsqrt(var + LN_EPS) * g + b).astype(z.dtype)


def _multiscale_pool(u_ext, start_pos, pool_w, pool_scale):
    b, tot, wdt = u_ext.shape
    n = tot - POOL_BUF
    u32 = u_ext.astype(jnp.float32)
    c0 = jnp.concatenate([jnp.zeros((b, 1, wdt), jnp.float32), jnp.cumsum(u32, axis=1)], axis=1)
    pos = start_pos + jnp.arange(n)
    u_new = u32[:, POOL_BUF:]
    outs = []
    for g, w in enumerate(POOL_WINDOWS):
        sl = slice(g * POOL_GROUP_WIDTH, (g + 1) * POOL_GROUP_WIDTH)
        s = c0[:, POOL_BUF + 1:POOL_BUF + 1 + n, sl] - c0[:, POOL_BUF + 1 - w:POOL_BUF + 1 - w + n, sl]
        cnt = jnp.minimum(w, pos + 1).astype(jnp.float32)[None, :, None]
        outs.append(s / cnt - u_new[..., sl])
    d = jnp.stack(outs, axis=2)
    y = jnp.einsum('btgc,gce->btge', d, pool_w.astype(jnp.float32)).reshape(b, n, POOL_WIDTH)
    return (y * pool_scale).astype(u_ext.dtype)


def _rel_bucket(n):
    max_exact = N_BUCKETS // 2
    nf = jnp.maximum(n, 1).astype(jnp.float32)
    large = max_exact + (jnp.log(nf / max_exact) / math.log(MAX_DISTANCE / max_exact)
                         * (N_BUCKETS - max_exact)).astype(jnp.int32)
    large = jnp.minimum(large, N_BUCKETS - 1)
    return jnp.where(n < max_exact, n, large)


def _dsa_attend(q, qi, wi, qpos, ki_all, gather_kv, topk, rel_bias):
    lk = ki_all.shape[1]
    s = jnp.einsum('bthd,bsd->bths', qi, ki_all).astype(jnp.float32) * (IDX_DIM ** -0.5)
    score = jnp.einsum('bth,bths->bts', wi.astype(jnp.float32) * (N_IDX_HEADS ** -0.5), jax.nn.relu(s))
    causal = jnp.arange(lk)[None, :] <= qpos[:, None]
    score = jnp.where(causal[None], score, -jnp.inf)
    _, idx = lax.top_k(score, topk)
    valid = idx <= qpos[None, :, None]
    kg, vg = gather_kv(idx)
    logits = jnp.einsum('bthd,btkhd->bthk', q, kg).astype(jnp.float32) * (HEAD_DIM ** -0.5)
    dist = jnp.maximum(qpos[None, :, None] - idx, 0)
    bias = jnp.moveaxis(rel_bias[_rel_bucket(dist)], -1, 2).astype(jnp.float32)
    logits = jnp.where(valid[:, :, None, :], logits + bias, -1e30)
    p = jax.nn.softmax(logits, axis=-1)
    return jnp.einsum('bthk,btkhd->bthd', p.astype(vg.dtype), vg)


def _merge(x, pool_o, g_pool, attn_o, g_attn, gate_p, gate_a,
           w_br_pool, w_br_attn, w_out, ln_g, ln_b):
    bp = jnp.einsum('btc,cd->btd', pool_o * jax.nn.silu(g_pool), w_br_pool)
    ba = jnp.einsum('btc,cd->btd', attn_o * jax.nn.silu(g_attn), w_br_attn)
    m = jax.nn.sigmoid(gate_p) * bp + jax.nn.sigmoid(gate_a) * ba
    out = jnp.einsum('btc,cd->btd', m, w_out)
    return _layer_norm(ALPHA * x + out, ln_g, ln_b)


def setup_inputs(seed: int = 0) -> dict:
    key = jax.random.key(seed)
    ks = jax.random.split(key, 20)
    n_pages = PAST_LEN // PAGE_SIZE
    in_use = DEC_BATCH * n_pages
    n_phys = in_use + max(1, in_use // 4)
    perm = jax.random.permutation(ks[0], n_phys)
    page_table = perm[:in_use].reshape(DEC_BATCH, n_pages).astype(jnp.int32)
    f = jnp.float32
    nrm = jax.random.normal
    return {
        'x_prompt': nrm(ks[1], (BATCH, SEQ, D_MODEL), f),
        'x_sample': nrm(ks[2], (DEC_BATCH, DEC_SEQ, D_MODEL), f),
        'cache_k': nrm(ks[3], (n_phys, PAGE_SIZE, N_HEADS, HEAD_DIM), f),
        'cache_v': nrm(ks[4], (n_phys, PAGE_SIZE, N_HEADS, HEAD_DIM), f),
        'cache_kidx': nrm(ks[5], (n_phys, PAGE_SIZE, IDX_DIM), f),
        'state_pool': nrm(ks[6], (DEC_BATCH, POOL_BUF, POOL_WIDTH), f),
        'page_table': page_table,
        'meta': nrm(ks[7], (N_META, D_MODEL), f),
        'w_in': nrm(ks[8], (D_MODEL, IN_WIDTH), f) * D_MODEL ** -0.5,
        'pool_w': nrm(ks[9], (POOL_GROUPS, POOL_GROUP_WIDTH, POOL_GROUP_WIDTH), f) * POOL_GROUP_WIDTH ** -0.5,
        'pool_scale': 1.0 + 0.02 * nrm(ks[10], (POOL_WIDTH,), f),
        'w_br_pool': nrm(ks[11], (POOL_WIDTH, D_MODEL), f) * (POOL_WIDTH ** -0.5) * BETA,
        'w_br_attn': nrm(ks[12], (ATTN_WIDTH, D_MODEL), f) * (ATTN_WIDTH ** -0.5) * BETA,
        'rel_bias': 0.5 * nrm(ks[13], (N_BUCKETS, N_HEADS), f),
        'w_out': nrm(ks[14], (D_MODEL, D_MODEL), f) * (D_MODEL ** -0.5) * BETA,
        'ln_g': 1.0 + 0.02 * nrm(ks[15], (D_MODEL,), f),
        'ln_b': 0.02 * nrm(ks[16], (D_MODEL,), f),
    }


def reference(x_prompt, x_sample, cache_k, cache_v, cache_kidx, state_pool, page_table,
              meta, w_in, pool_w, pool_scale, w_br_pool, w_br_attn, rel_bias, w_out, ln_g, ln_b):
    take = jax.vmap(lambda a, i: a[i])

    bsz, seq, _ = x_prompt.shape
    x = jnp.concatenate([jnp.broadcast_to(meta[None].astype(x_prompt.dtype), (bsz, N_META, D_MODEL)),
                         x_prompt], axis=1)
    L = seq + N_META
    for _layer in range(DEPTH):
        u, g_pool, q, k, v, g_attn, qi, ki, wi, gate_p, gate_a = _project(x, w_in)
        u_ext = jnp.pad(u, ((0, 0), (POOL_BUF, 0), (0, 0)))
        pool_o = _multiscale_pool(u_ext, 0, pool_w, pool_scale)
        pool_prompt = u_ext[:, -POOL_BUF:]
        q = q.reshape(bsz, L, N_HEADS, HEAD_DIM)
        k_prompt = k.reshape(bsz, L, N_HEADS, HEAD_DIM)
        v_prompt = v.reshape(bsz, L, N_HEADS, HEAD_DIM)
        qi = qi.reshape(bsz, L, N_IDX_HEADS, IDX_DIM)
        kidx_prompt = ki
        topk_p = min(TOPK_MAX, seq // 4)

        def gather_p(idx):
            return take(k_prompt, idx), take(v_prompt, idx)

        nb = -(-L // Q_BLOCK)
        pad = nb * Q_BLOCK - L

        def to_blocks(a):
            a = jnp.pad(a, ((0, 0), (0, pad)) + ((0, 0),) * (a.ndim - 2))
            return jnp.moveaxis(a.reshape((bsz, nb, Q_BLOCK) + a.shape[2:]), 1, 0)

        pos_b = jnp.arange(nb * Q_BLOCK).reshape(nb, Q_BLOCK)

        def attend_block(args):
            qq, qqi, qwi, qp = args
            return _dsa_attend(qq, qqi, qwi, qp, kidx_prompt, gather_p, topk_p, rel_bias)

        ob = lax.map(attend_block, (to_blocks(q), to_blocks(qi), to_blocks(wi), pos_b))
        attn_o = jnp.moveaxis(ob, 0, 1).reshape(bsz, nb * Q_BLOCK, ATTN_WIDTH)[:, :L]
        x = _merge(x, pool_o, g_pool, attn_o, g_attn, gate_p, gate_a,
                   w_br_pool, w_br_attn, w_out, ln_g, ln_b)
    y_prompt = x[:, N_META:]

    dbsz, tnew, _ = x_sample.shape
    page = cache_k.shape[1]
    past_len = page_table.shape[1] * page
    xs = x_sample
    for _layer in range(DEPTH):
        u, g_pool, q, k, v, g_attn, qi, ki, wi, gate_p, gate_a = _project(xs, w_in)
        u_ext = jnp.concatenate([state_pool.astype(u.dtype), u], axis=1)
        pool_o = _multiscale_pool(u_ext, past_len, pool_w, pool_scale)
        pool_sample = u_ext[:, -POOL_BUF:]
        q = q.reshape(dbsz, tnew, N_HEADS, HEAD_DIM)
        k_sample = k.reshape(dbsz, tnew, N_HEADS, HEAD_DIM)
        v_sample = v.reshape(dbsz, tnew, N_HEADS, HEAD_DIM)
        qi = qi.reshape(dbsz, tnew, N_IDX_HEADS, IDX_DIM)
        kidx_sample = ki
        ki_past = cache_kidx[page_table].reshape(dbsz, past_len, IDX_DIM)
        ki_all = jnp.concatenate([ki_past.astype(ki.dtype), ki], axis=1)
        topk_s = min(TOPK_MAX, (past_len + tnew) // 4)
        k_flat = cache_k.reshape(-1, N_HEADS, HEAD_DIM)
        v_flat = cache_v.reshape(-1, N_HEADS, HEAD_DIM)

        def gather_s(idx):
            in_past = (idx < past_len)[..., None, None]
            ic = jnp.minimum(idx, past_len - 1)
            phys = take(page_table, ic // page)
            rows = phys * page + ic % page
            jn = jnp.clip(idx - past_len, 0, tnew - 1)
            kg = jnp.where(in_past, k_flat[rows].astype(k_sample.dtype), take(k_sample, jn))
            vg = jnp.where(in_past, v_flat[rows].astype(v_sample.dtype), take(v_sample, jn))
            return kg, vg

        qpos_s = past_len + jnp.arange(tnew)
        attn_o = _dsa_attend(q, qi, wi, qpos_s, ki_all, gather_s, topk_s, rel_bias)
        attn_o = attn_o.reshape(dbsz, tnew, ATTN_WIDTH)
        xs = _merge(xs, pool_o, g_pool, attn_o, g_attn, gate_p, gate_a,
                    w_br_pool, w_br_attn, w_out, ln_g, ln_b)
    y_sample = xs

    return (y_prompt, y_sample, k_prompt, v_prompt, kidx_prompt, pool_prompt,
            k_sample, v_sample, kidx_sample, pool_sample)
```

```python
import functools
import math

import numpy as np
import jax
import jax.numpy as jnp
from jax import lax
from jax.experimental import pallas as pl
from jax.experimental.pallas import tpu as pltpu

f32 = jnp.float32
bf16 = jnp.bfloat16
i32 = jnp.int32

N_META = 16
POOL_WINDOWS = (2, 4, 8, 16)
POOL_BUF = max(POOL_WINDOWS) - 1
N_HEADS = 8
HEAD_DIM = 64
N_IDX_HEADS = 4
IDX_DIM = 64
TOPK_MAX = 256
N_BUCKETS = 32
MAX_DISTANCE = 128
DEPTH = 1
ALPHA = (2 * DEPTH) ** 0.25
LN_EPS = 1e-5
MASKED_LOGIT = -1e30

LANES = 128
SUBLANES = 8
WORD_BITS = 32
VMEM_LIMIT = 56 * 1024 * 1024

TQ = 256
GROUP_KEYS = WORD_BITS * LANES
INT_MIN = -(2 ** 31)

D_MODEL = 1024
POOL_WIDTH = 512
ATTN_WIDTH = N_HEADS * HEAD_DIM
PAIR = 2 * HEAD_DIM

SEG = {
    "u": (0, 512), "gp": (512, 512), "q": (1024, 512), "k": (1536, 512), "v": (2048, 512),
    "ga": (2560, 512), "qi": (3072, 256), "ki2": (3328, 128), "wi": (3456, 128),
    "gate_p": (3584, 1024), "gate_a": (4608, 1024),
}
W_COLS = 5632


def _cparams(n_axes):
    return pltpu.CompilerParams(dimension_semantics=("arbitrary",) * n_axes,
                                vmem_limit_bytes=VMEM_LIMIT)


def _project_body(x_ref, w_ref, u_o, gp_o, q_o, k_o, kb_o, v_o, vb_o, ga_o, qi_o, kidx_o, kib_o,
                  wi_o, gtp_o, gta_o):
    xb = x_ref[...].astype(bf16)

    def seg(name):
        off, n = SEG[name]
        return jnp.dot(xb, w_ref[:, off:off + n], preferred_element_type=f32)

    u_o[...] = seg("u")
    gp_o[...] = seg("gp")
    q_o[...] = seg("q").astype(bf16)
    k = seg("k")
    k_o[...] = k
    kb_o[...] = k.astype(bf16)
    v = seg("v")
    v_o[...] = v
    vb_o[...] = v.astype(bf16)
    ga_o[...] = seg("ga")
    qi_o[...] = seg("qi").astype(bf16)
    ki2 = seg("ki2")
    kidx_o[...] = ki2[:, :IDX_DIM]
    kib_o[...] = ki2.astype(bf16)
    wi_o[...] = seg("wi")
    gtp_o[...] = seg("gate_p")
    gta_o[...] = seg("gate_a")


def _project(x2d, w_pad, tm):
    rows = x2d.shape[0]
    assert rows % tm == 0
    widths = [(512, f32), (512, f32), (512, bf16), (512, f32), (512, bf16), (512, f32), (512, bf16),
              (512, f32), (256, bf16), (IDX_DIM, f32), (128, bf16), (128, f32), (1024, f32), (1024, f32)]
    out_shape = [jax.ShapeDtypeStruct((rows, n), dt) for n, dt in widths]
    out_specs = [pl.BlockSpec((tm, n), lambda i: (i, 0)) for n, _ in widths]
    return pl.pallas_call(
        _project_body,
        grid=(rows // tm,),
        in_specs=[pl.BlockSpec((tm, D_MODEL), lambda i: (i, 0)),
                  pl.BlockSpec((D_MODEL, W_COLS), lambda i: (0, 0))],
        out_specs=out_specs,
        out_shape=out_shape,
        compiler_params=_cparams(1),
    )(x2d, w_pad)


def _score_key(score):
    bits = lax.bitcast_convert_type(score, i32)
    sign = lax.shift_right_arithmetic(bits, jnp.full(bits.shape, 31, i32))
    mag = bits & jnp.int32(0x7FFFFFFF)
    return (mag ^ sign) - sign


def _bit_transpose32(words):
    a = list(words)
    j = 16
    m = 0x0000FFFF
    while j != 0:
        k = 0
        sh = jnp.full(a[0].shape, j, i32)
        mm = jnp.int32(np.uint32(m).astype(np.int32))
        while k < 32:
            t = (a[k] ^ lax.shift_right_logical(a[k + j], sh)) & mm
            a[k] = a[k] ^ t
            a[k + j] = a[k + j] ^ lax.shift_left(t, sh)
            k = (k + j + 1) & ~j
        j >>= 1
        if j:
            m = (m ^ (m << j)) & 0xFFFFFFFF
    return a


def _transpose_group(planes_scr, g, rows_total):
    def body(rg, carry):
        rows = pl.ds(pl.multiple_of(rg * SUBLANES, SUBLANES), SUBLANES)
        words = [planes_scr[g, WORD_BITS - 1 - c, rows, :] for c in range(WORD_BITS)]
        t = _bit_transpose32(words)
        for kk in range(WORD_BITS):
            b = WORD_BITS - 1 - kk
            planes_scr[g, b, rows, :] = ~t[kk] if b == WORD_BITS - 1 else t[kk]
        return carry

    lax.fori_loop(0, rows_total // SUBLANES, body, 0)


def _lane_total(cnt, ones_mat):
    return jnp.dot(cnt.astype(f32).astype(bf16), ones_mat, preferred_element_type=f32)


def _radix_select(planes_scr, single_key, alive_scr, sel_scr, krem_scr, n_groups, single_first,
                  alive_init, k_row):
    rows = single_key.shape[0]
    nset = n_groups + 1
    ones_mat = jnp.ones((LANES, LANES), bf16)
    zero = jnp.zeros((rows, LANES), i32)

    for s in range(nset):
        alive_scr[s] = alive_init[s]
        sel_scr[s] = zero
    krem_scr[...] = k_row

    def step(planes):
        alive = [alive_scr[s] for s in range(nset)]
        ones = [alive[s] & planes[s] for s in range(nset)]
        cnt = ones[0]
        for s in range(1, nset):
            cnt = cnt + lax.population_count(ones[s])
        tot = _lane_total(cnt, ones_mat)
        krem = krem_scr[...]
        take = tot >= krem
        for s in range(nset):
            alive_scr[s] = jnp.where(take, ones[s], alive[s] ^ ones[s])
            sel_scr[s] = sel_scr[s] | jnp.where(take, zero, ones[s])
        krem_scr[...] = jnp.where(take, krem, krem - tot)

    def value_step(t, carry):
        b = WORD_BITS - 1 - t
        sh = jnp.full((rows, LANES), b, i32)
        planes = [lax.shift_right_logical(single_key, sh) & 1]
        for g in range(n_groups):
            planes.append(planes_scr[g, b])
        step(planes)
        return carry

    lax.fori_loop(0, WORD_BITS, value_step, 0)

    full = jnp.full((rows, LANES), -1, i32)
    one = jnp.full((rows, LANES), 1, i32)
    codes = [0] + [g + 1 for g in range(n_groups)] if single_first else [n_groups] + list(range(n_groups))
    n_code_bits = max(1, (max(codes)).bit_length())
    for cb in reversed(range(n_code_bits)):
        planes = []
        for s in range(nset):
            pref = ((codes[s] >> cb) & 1) == 0
            planes.append((one if s == 0 else full) if pref else zero)
        step(planes)
    chunk_masks = (0x0000FFFF, 0x00FF00FF, 0x0F0F0F0F, 0x33333333, 0x55555555)
    for cm in chunk_masks:
        word = jnp.full((rows, LANES), np.uint32(cm).astype(np.int32), i32)
        step([one] + [word] * n_groups)
    lane = lax.broadcasted_iota(i32, (rows, LANES), 1)
    for lb in reversed(range(7)):
        low = ((lane >> lb) & 1) == 0
        lw = jnp.where(low, full, zero)
        step([lw & 1] + [lw] * n_groups)
    for s in range(nset):
        sel_scr[s] = sel_scr[s] | alive_scr[s]


def _mask_from_bits(word, bit):
    sh = jnp.full(word.shape, bit, i32)
    picked = (lax.shift_right_logical(word, sh) & 1) != 0
    return jnp.where(picked, 0.0, MASKED_LOGIT).astype(f32)


def _half_masks(x_pair, lane_lo):
    xf = x_pair.astype(f32)
    z = jnp.zeros_like(xf)
    return (jnp.where(lane_lo, xf, z).astype(x_pair.dtype),
            jnp.where(lane_lo, z, xf).astype(x_pair.dtype))


def _online_softmax_update(s, h, m_scr, l_scr, ones_cols):
    reps = s.shape[1] // LANES
    m_old = m_scr[h]
    m_new = jnp.maximum(m_old, jnp.max(s, axis=-1, keepdims=True))
    alpha = jnp.exp(m_old - m_new)
    m_rep = m_new if reps == 1 else jnp.concatenate([m_new] * reps, axis=1)
    p = jnp.exp(s - m_rep).astype(bf16)
    l_scr[h] = alpha * l_scr[h] + jnp.dot(p, ones_cols, preferred_element_type=f32)
    m_scr[h] = m_new
    return alpha, p


def _attn_prompt_body(qi_ref, wi_ref, q_ref, ki_ref, k_ref, v_ref, kim_ref, km_ref, vm_ref,
                      bpp_ref, bmeta_ref, o_ref,
                      planes_scr, alive_scr, sel_scr, krem_scr, m_scr, l_scr, acc_scr,
                      *, n_groups, topk):
    i = pl.program_id(1)
    lane = lax.broadcasted_iota(i32, (TQ, LANES), 1)
    lane_lo = lane < HEAD_DIM
    NT = (((1,), (1,)), ((), ()))

    qi = qi_ref[0]
    qim = []
    for hp in range(N_IDX_HEADS // 2):
        qim.extend(_half_masks(qi[:, PAIR * hp:PAIR * (hp + 1)], lane_lo))
    wi = wi_ref[0]
    wcol = [wi[:, h:h + 1] for h in range(N_IDX_HEADS)]

    def score_block(kt2):
        acc = None
        for h in range(N_IDX_HEADS):
            s = lax.dot_general(qim[h], kt2, NT, preferred_element_type=f32)
            t = jnp.maximum(s, 0.0) * wcol[h]
            acc = t if acc is None else acc + t
        return acc

    def store_keys(j, key):
        c0 = 2 * j
        g = c0 // WORD_BITS
        jj = c0 % WORD_BITS
        planes_scr[g, jj] = key[:, :LANES]
        planes_scr[g, jj + 1] = key[:, LANES:]

    def score_tile(j, carry):
        kt2 = ki_ref[0, pl.ds(pl.multiple_of(j * TQ, TQ), TQ), :]
        store_keys(j, _score_key(score_block(kt2)))
        return carry

    lax.fori_loop(0, i, score_tile, 0)
    row2 = lax.broadcasted_iota(i32, (TQ, TQ), 0)
    col2 = lax.broadcasted_iota(i32, (TQ, TQ), 1)
    kt2 = ki_ref[0, pl.ds(pl.multiple_of(i * TQ, TQ), TQ), :]
    key_d = jnp.where(col2 <= row2, _score_key(score_block(kt2)), INT_MIN)
    store_keys(i, key_d)

    n_chunks = 2 * (i + 1)
    groups_used = (n_chunks + WORD_BITS - 1) // WORD_BITS
    fill = jnp.full((TQ, LANES), INT_MIN, i32)

    def fill_chunk(c, carry):
        planes_scr[c // WORD_BITS, c % WORD_BITS] = fill
        return carry

    lax.fori_loop(n_chunks, groups_used * WORD_BITS, fill_chunk, 0)

    key_m = _score_key(score_block(kim_ref[...]))
    key_m = jnp.where(lane < N_META, key_m, INT_MIN) ^ jnp.int32(INT_MIN)

    for g in range(n_groups):
        @pl.when(g < groups_used)
        def _():
            _transpose_group(planes_scr, g, TQ)

    full = jnp.full((TQ, LANES), -1, i32)
    zero = jnp.zeros((TQ, LANES), i32)
    alive_init = [jnp.full((TQ, LANES), 1, i32)]
    for g in range(n_groups):
        alive_init.append(jnp.where(g < groups_used, full, zero))
    row = lax.broadcasted_iota(i32, (TQ, LANES), 0)
    n_causal = i * TQ + row + (N_META + 1)
    k_row = jnp.minimum(n_causal, topk).astype(f32)
    _radix_select(planes_scr, key_m, alive_scr, sel_scr, krem_scr, n_groups, True, alive_init, k_row)

    q = q_ref[0]
    qm = []
    for hp in range(N_HEADS // 2):
        qm.extend(_half_masks(q[:, PAIR * hp:PAIR * (hp + 1)], lane_lo))
    for h in range(N_HEADS):
        m_scr[h] = jnp.full((TQ, LANES), -jnp.inf, f32)
        l_scr[h] = jnp.zeros((TQ, LANES), f32)
    for hp in range(N_HEADS // 2):
        acc_scr[hp] = jnp.zeros((TQ, LANES), f32)

    def attend(kblk_fn, vblk_fn, maskadd, bias_fn, width):
        ones_cols = jnp.ones((width, LANES), bf16)
        for hp in range(N_HEADS // 2):
            kp = kblk_fn(hp)
            vp = vblk_fn(hp)
            al, pv = [], []
            for e in range(2):
                h = 2 * hp + e
                s = lax.dot_general(qm[h], kp, NT, preferred_element_type=f32)
                s = s + bias_fn(h) + maskadd
                alpha, p = _online_softmax_update(s, h, m_scr, l_scr, ones_cols)
                al.append(alpha)
                pv.append(jnp.dot(p, vp, preferred_element_type=f32))
            acc_scr[hp] = (jnp.where(lane_lo, al[0], al[1]) * acc_scr[hp]
                           + jnp.where(lane_lo, pv[0], pv[1]))

    near_m = jnp.minimum(i, 1)
    attend(lambda hp: km_ref[:, PAIR * hp:PAIR * (hp + 1)],
           lambda hp: vm_ref[:, PAIR * hp:PAIR * (hp + 1)],
           _mask_from_bits(sel_scr[0], 0),
           lambda h: bmeta_ref[near_m, h], LANES)

    def attend_tile(j, carry):
        c0 = 2 * j
        word = sel_scr[1 + c0 // WORD_BITS]
        jj = c0 % WORD_BITS
        maskadd = jnp.concatenate([_mask_from_bits(word, jj), _mask_from_bits(word, jj + 1)], axis=1)
        rows = pl.ds(pl.multiple_of(j * TQ, TQ), TQ)
        dd = jnp.minimum(i - j, 2)
        attend(lambda hp: k_ref[0, rows, PAIR * hp:PAIR * (hp + 1)],
               lambda hp: v_ref[0, rows, PAIR * hp:PAIR * (hp + 1)],
               maskadd, lambda h: bpp_ref[dd, h], TQ)
        return carry

    lax.fori_loop(0, i + 1, attend_tile, 0)

    for hp in range(N_HEADS // 2):
        l_c = jnp.where(lane_lo, l_scr[2 * hp], l_scr[2 * hp + 1])
        o_ref[0, :, PAIR * hp:PAIR * (hp + 1)] = acc_scr[hp] / l_c


def _attn_prompt(qi, wi, q, kib, kb, vb, kim, km, vm, bias_pp, bias_meta, topk):
    bsz, seq, _ = q.shape
    assert seq % TQ == 0
    n_groups = -(-seq // GROUP_KEYS)
    nq = seq // TQ
    const = lambda nd: (lambda b, i: (0,) * nd)
    body = functools.partial(_attn_prompt_body, n_groups=n_groups, topk=topk)
    return pl.pallas_call(
        body,
        grid=(bsz, nq),
        in_specs=[
            pl.BlockSpec((1, TQ, N_IDX_HEADS * IDX_DIM), lambda b, i: (b, i, 0)),
            pl.BlockSpec((1, TQ, LANES), lambda b, i: (b, i, 0)),
            pl.BlockSpec((1, TQ, ATTN_WIDTH), lambda b, i: (b, i, 0)),
            pl.BlockSpec((1, seq, LANES), lambda b, i: (b, 0, 0), pipeline_mode=pl.Buffered(1)),
            pl.BlockSpec((1, seq, ATTN_WIDTH), lambda b, i: (b, 0, 0), pipeline_mode=pl.Buffered(1)),
            pl.BlockSpec((1, seq, ATTN_WIDTH), lambda b, i: (b, 0, 0), pipeline_mode=pl.Buffered(1)),
            pl.BlockSpec((LANES, LANES), const(2)),
            pl.BlockSpec((LANES, ATTN_WIDTH), const(2)),
            pl.BlockSpec((LANES, ATTN_WIDTH), const(2)),
            pl.BlockSpec((3, N_HEADS, TQ, TQ), const(4), pipeline_mode=pl.Buffered(1)),
            pl.BlockSpec((2, N_HEADS, TQ, LANES), const(4), pipeline_mode=pl.Buffered(1)),
        ],
        out_specs=pl.BlockSpec((1, TQ, ATTN_WIDTH), lambda b, i: (b, i, 0)),
        out_shape=jax.ShapeDtypeStruct((bsz, seq, ATTN_WIDTH), f32),
        scratch_shapes=[
            pltpu.VMEM((n_groups, WORD_BITS, TQ, LANES), i32),
            pltpu.VMEM((n_groups + 1, TQ, LANES), i32),
            pltpu.VMEM((n_groups + 1, TQ, LANES), i32),
            pltpu.VMEM((TQ, LANES), f32),
            pltpu.VMEM((N_HEADS, TQ, LANES), f32),
            pltpu.VMEM((N_HEADS, TQ, LANES), f32),
            pltpu.VMEM((N_HEADS // 2, TQ, LANES), f32),
        ],
        compiler_params=_cparams(2),
    )(qi, wi, q, kib, kb, vb, kim, km, vm, bias_pp, bias_meta)


PAGES_PER_STEP = 8
SROWS = 8
KROWS = 16


def _sample_scores_body(pt_ref, qi_ref, wb_ref, kin_ref, *rest):
    page_refs = rest[:PAGES_PER_STEP]
    out_ref, outn_ref = rest[PAGES_PER_STEP:]
    NT = (((1,), (1,)), ((), ()))
    qi = qi_ref[0]
    wb = wb_ref[0]

    def scores(kt2):
        s = lax.dot_general(qi, kt2, NT, preferred_element_type=f32)
        t = jnp.maximum(s, 0.0) * wb
        acc = t[0:SROWS]
        for h in range(1, N_IDX_HEADS):
            acc = acc + t[SROWS * h:SROWS * (h + 1)]
        return acc

    for jx in range(PAGES_PER_STEP):
        kp = page_refs[jx][0].astype(bf16)
        out_ref[0, :, LANES * jx:LANES * (jx + 1)] = scores(jnp.concatenate([kp, kp], axis=1))
    outn_ref[0] = scores(kin_ref[0])


def _sample_scores(page_table, qi32, wb32, kin2, cache_kidx):
    dbsz, n_pages = page_table.shape
    page = cache_kidx.shape[1]
    assert page == LANES and n_pages % PAGES_PER_STEP == 0
    nsteps = n_pages // PAGES_PER_STEP
    mrows = N_IDX_HEADS * SROWS

    def page_spec(jx):
        return pl.BlockSpec((1, page, IDX_DIM), lambda b, s, pt: (pt[b, s * PAGES_PER_STEP + jx], 0, 0))

    grid_spec = pltpu.PrefetchScalarGridSpec(
        num_scalar_prefetch=1,
        grid=(dbsz, nsteps),
        in_specs=[pl.BlockSpec((1, mrows, LANES), lambda b, s, pt: (b, 0, 0)),
                  pl.BlockSpec((1, mrows, LANES), lambda b, s, pt: (b, 0, 0)),
                  pl.BlockSpec((1, LANES, LANES), lambda b, s, pt: (b, 0, 0))]
                 + [page_spec(jx) for jx in range(PAGES_PER_STEP)],
        out_specs=[pl.BlockSpec((1, SROWS, PAGES_PER_STEP * LANES), lambda b, s, pt: (b, 0, s)),
                   pl.BlockSpec((1, SROWS, LANES), lambda b, s, pt: (b, 0, 0))],
    )
    return pl.pallas_call(
        _sample_scores_body,
        grid_spec=grid_spec,
        out_shape=[jax.ShapeDtypeStruct((dbsz, SROWS, n_pages * page), f32),
                   jax.ShapeDtypeStruct((dbsz, SROWS, LANES), f32)],
        compiler_params=_cparams(2),
    )(page_table, qi32, wb32, kin2, *([cache_kidx] * PAGES_PER_STEP))


SEL_ROWS = 128


def _sample_select_body(sp_ref, sn_ref, mp_ref, mn_ref, planes_scr, alive_scr, sel_scr, krem_scr,
                        *, n_groups, n_new, topk):
    n_chunks = sp_ref.shape[1] // LANES

    def key_chunk(c, carry):
        col = pl.ds(pl.multiple_of(c * LANES, LANES), LANES)
        planes_scr[c // WORD_BITS, c % WORD_BITS] = _score_key(sp_ref[:, col])
        return carry

    lax.fori_loop(0, n_chunks, key_chunk, 0)
    for c in range(n_chunks, n_groups * WORD_BITS):
        planes_scr[c // WORD_BITS, c % WORD_BITS] = jnp.full((SEL_ROWS, LANES), INT_MIN, i32)
    lane = lax.broadcasted_iota(i32, (SEL_ROWS, LANES), 1)
    row = lax.broadcasted_iota(i32, (SEL_ROWS, LANES), 0)
    causal_new = (lane <= (row & (SROWS - 1))) & (lane < n_new)
    key_n = jnp.where(causal_new, _score_key(sn_ref[...]), INT_MIN) ^ jnp.int32(INT_MIN)

    for g in range(n_groups):
        _transpose_group(planes_scr, g, SEL_ROWS)
    full = jnp.full((SEL_ROWS, LANES), -1, i32)
    alive_init = [jnp.full((SEL_ROWS, LANES), 1, i32)] + [full] * n_groups
    k_row = jnp.full((SEL_ROWS, LANES), topk, f32)
    _radix_select(planes_scr, key_n, alive_scr, sel_scr, krem_scr, n_groups, False, alive_init, k_row)

    def mask_chunk(c, carry):
        col = pl.ds(pl.multiple_of(c * LANES, LANES), LANES)
        mp_ref[:, col] = _mask_from_bits(sel_scr[1 + c // WORD_BITS], c % WORD_BITS)
        return carry

    lax.fori_loop(0, n_chunks, mask_chunk, 0)
    mn_ref[...] = _mask_from_bits(sel_scr[0], 0)


def _sample_select(scores_past, scores_new, n_new, topk):
    rows, past = scores_past.shape
    assert rows % SEL_ROWS == 0 and past % LANES == 0
    n_groups = -(-past // GROUP_KEYS)
    body = functools.partial(_sample_select_body, n_groups=n_groups, n_new=n_new, topk=topk)
    return pl.pallas_call(
        body,
        grid=(rows // SEL_ROWS,),
        in_specs=[pl.BlockSpec((SEL_ROWS, past), lambda r: (r, 0)),
                  pl.BlockSpec((SEL_ROWS, LANES), lambda r: (r, 0))],
        out_specs=[pl.BlockSpec((SEL_ROWS, past), lambda r: (r, 0)),
                   pl.BlockSpec((SEL_ROWS, LANES), lambda r: (r, 0))],
        out_shape=[jax.ShapeDtypeStruct((rows, past), f32),
                   jax.ShapeDtypeStruct((rows, LANES), f32)],
        scratch_shapes=[
            pltpu.VMEM((n_groups, WORD_BITS, SEL_ROWS, LANES), i32),
            pltpu.VMEM((n_groups + 1, SEL_ROWS, LANES), i32),
            pltpu.VMEM((n_groups + 1, SEL_ROWS, LANES), i32),
            pltpu.VMEM((SEL_ROWS, LANES), f32),
        ],
        compiler_params=_cparams(1),
    )(scores_past, scores_new)


def _sample_attend_body(pt_ref, q_ref, mp_ref, mn_ref, kn_ref, vn_ref, bs_ref, bn_ref, *rest,
                        n_pages):
    k_refs = rest[:PAGES_PER_STEP]
    v_refs = rest[PAGES_PER_STEP:2 * PAGES_PER_STEP]
    o_ref, m_scr, l_scr, acc_scr = rest[2 * PAGES_PER_STEP:]
    s_idx = pl.program_id(1)
    nsteps = pl.num_programs(1)
    NT = (((1,), (1,)), ((), ()))
    mrows = N_HEADS * SROWS
    q = q_ref[0]

    @pl.when(s_idx == 0)
    def _():
        m_scr[0] = jnp.full((mrows, LANES), -jnp.inf, f32)
        l_scr[0] = jnp.zeros((mrows, LANES), f32)
        acc_scr[...] = jnp.zeros((mrows, ATTN_WIDTH), f32)

    ones_cols = jnp.ones((LANES, LANES), bf16)

    def tile_rows(x):
        return jnp.concatenate([x] * N_HEADS, axis=0)

    def attend(kp, vp, bias, maskadd):
        s = lax.dot_general(q, kp, NT, preferred_element_type=f32) + bias + tile_rows(maskadd)
        alpha, p = _online_softmax_update(s, 0, m_scr, l_scr, ones_cols)
        alpha_w = jnp.concatenate([alpha] * (ATTN_WIDTH // LANES), axis=1)
        acc_scr[...] = alpha_w * acc_scr[...] + jnp.dot(p, vp, preferred_element_type=f32)

    for jx in range(PAGES_PER_STEP):
        pg = s_idx * PAGES_PER_STEP + jx
        near = (pg == n_pages - 1).astype(i32)
        attend(k_refs[jx][0].astype(bf16), v_refs[jx][0].astype(bf16), bs_ref[near],
               mp_ref[0, :, LANES * jx:LANES * (jx + 1)])

    @pl.when(s_idx == nsteps - 1)
    def _():
        pad = jnp.zeros((LANES - KROWS, ATTN_WIDTH), bf16)
        attend(jnp.concatenate([kn_ref[0], pad], axis=0), jnp.concatenate([vn_ref[0], pad], axis=0),
               bn_ref[...], mn_ref[0])
        l_w = jnp.concatenate([l_scr[0]] * (ATTN_WIDTH // LANES), axis=1)
        accn = acc_scr[...] / l_w
        head_of_lane = lax.broadcasted_iota(i32, (SROWS, ATTN_WIDTH), 1) // HEAD_DIM
        out = jnp.zeros((SROWS, ATTN_WIDTH), f32)
        for h in range(N_HEADS):
            out = out + jnp.where(head_of_lane == h, accn[SROWS * h:SROWS * (h + 1)], 0.0)
        o_ref[0] = out


def _sample_attend(page_table, q64, mask_past, mask_new, kn8, vn8, bias_s, bias_n, cache_k2, cache_v2):
    dbsz, n_pages = page_table.shape
    page = cache_k2.shape[1]
    nsteps = n_pages // PAGES_PER_STEP
    mrows = N_HEADS * SROWS

    def page_spec(jx):
        return pl.BlockSpec((1, page, ATTN_WIDTH), lambda b, s, pt: (pt[b, s * PAGES_PER_STEP + jx], 0, 0))

    grid_spec = pltpu.PrefetchScalarGridSpec(
        num_scalar_prefetch=1,
        grid=(dbsz, nsteps),
        in_specs=[pl.BlockSpec((1, mrows, ATTN_WIDTH), lambda b, s, pt: (b, 0, 0)),
                  pl.BlockSpec((1, SROWS, PAGES_PER_STEP * LANES), lambda b, s, pt: (b, 0, s)),
                  pl.BlockSpec((1, SROWS, LANES), lambda b, s, pt: (b, 0, 0)),
                  pl.BlockSpec((1, KROWS, ATTN_WIDTH), lambda b, s, pt: (b, 0, 0)),
                  pl.BlockSpec((1, KROWS, ATTN_WIDTH), lambda b, s, pt: (b, 0, 0)),
                  pl.BlockSpec((2, mrows, LANES), lambda b, s, pt: (0, 0, 0)),
                  pl.BlockSpec((mrows, LANES), lambda b, s, pt: (0, 0))]
                 + [page_spec(jx) for jx in range(PAGES_PER_STEP)]
                 + [page_spec(jx) for jx in range(PAGES_PER_STEP)],
        out_specs=pl.BlockSpec((1, SROWS, ATTN_WIDTH), lambda b, s, pt: (b, 0, 0)),
        scratch_shapes=[pltpu.VMEM((1, mrows, LANES), f32),
                        pltpu.VMEM((1, mrows, LANES), f32),
                        pltpu.VMEM((mrows, ATTN_WIDTH), f32)],
    )
    body = functools.partial(_sample_attend_body, n_pages=n_pages)
    return pl.pallas_call(
        body,
        grid_spec=grid_spec,
        out_shape=jax.ShapeDtypeStruct((dbsz, SROWS, ATTN_WIDTH), f32),
        compiler_params=_cparams(2),
    )(page_table, q64, mask_past, mask_new, kn8, vn8, bias_s, bias_n,
      *([cache_k2] * PAGES_PER_STEP), *([cache_v2] * PAGES_PER_STEP))


def _silu(x):
    return x * jax.nn.sigmoid(x)


def _merge_core(x, d, gp, ao, ga, gtp, gta, pw_ref, ps_ref, wbp_ref, wba_ref, wo_ref, lng_ref, lnb_ref):
    gw = POOL_WIDTH // len(POOL_WINDOWS)
    parts = [jnp.dot(d[:, gw * g:gw * (g + 1)].astype(bf16), pw_ref[g], preferred_element_type=f32)
             for g in range(len(POOL_WINDOWS))]
    pool_o = jnp.concatenate(parts, axis=1) * ps_ref[...]
    bp = jnp.dot((pool_o * _silu(gp)).astype(bf16), wbp_ref[...], preferred_element_type=f32)
    ba = jnp.dot((ao * _silu(ga)).astype(bf16), wba_ref[...], preferred_element_type=f32)
    m = jax.nn.sigmoid(gtp) * bp + jax.nn.sigmoid(gta) * ba
    out = jnp.dot(m.astype(bf16), wo_ref[...], preferred_element_type=f32)
    z = ALPHA * x + out
    mu = jnp.mean(z, axis=-1, keepdims=True)
    zc = z - mu
    var = jnp.mean(zc * zc, axis=-1, keepdims=True)
    return zc * lax.rsqrt(var + LN_EPS) * lng_ref[...] + lnb_ref[...]


def _merge_prompt_body(x_ref, u_ref, uh_ref, um_ref, gp_ref, ao_ref, ga_ref, gtp_ref, gta_ref,
                       pw_ref, ps_ref, wbp_ref, wba_ref, wo_ref, lng_ref, lnb_ref, y_ref, uu_scr):
    t = pl.program_id(1)
    halo = N_META
    uu_scr[0:halo, :] = jnp.where(t == 0, um_ref[...], uh_ref[0])
    uu_scr[halo:halo + TQ, :] = u_ref[0]
    gw = POOL_WIDTH // len(POOL_WINDOWS)
    parts = []
    for g, w in enumerate(POOL_WINDOWS):
        cols = slice(gw * g, gw * (g + 1))
        s = uu_scr[halo:halo + TQ, cols]
        for j in range(1, w):
            s = s + uu_scr[halo - j:halo - j + TQ, cols]
        parts.append(s * (1.0 / w) - uu_scr[halo:halo + TQ, cols])
    d = jnp.concatenate(parts, axis=1)
    y_ref[0] = _merge_core(x_ref[0], d, gp_ref[0], ao_ref[0], ga_ref[0], gtp_ref[0], gta_ref[0],
                           pw_ref, ps_ref, wbp_ref, wba_ref, wo_ref, lng_ref, lnb_ref)


def _weight_specs(nd_grid):
    z = lambda nd: (lambda *a: (0,) * nd)
    gw = POOL_WIDTH // len(POOL_WINDOWS)
    return [pl.BlockSpec((len(POOL_WINDOWS), gw, gw), z(3)),
            pl.BlockSpec((1, POOL_WIDTH), z(2)),
            pl.BlockSpec((POOL_WIDTH, D_MODEL), z(2)),
            pl.BlockSpec((ATTN_WIDTH, D_MODEL), z(2)),
            pl.BlockSpec((D_MODEL, D_MODEL), z(2)),
            pl.BlockSpec((1, D_MODEL), z(2)),
            pl.BlockSpec((1, D_MODEL), z(2))]


def _merge_prompt(x, u, u_meta, gp, ao, ga, gtp, gta, weights):
    bsz, seq, _ = x.shape
    nt = seq // TQ
    hb = TQ // N_META
    tile = lambda n: pl.BlockSpec((1, TQ, n), lambda b, t: (b, t, 0))
    return pl.pallas_call(
        _merge_prompt_body,
        grid=(bsz, nt),
        in_specs=[tile(D_MODEL), tile(POOL_WIDTH),
                  pl.BlockSpec((1, N_META, POOL_WIDTH), lambda b, t: (b, jnp.maximum(t * hb - 1, 0), 0)),
                  pl.BlockSpec((N_META, POOL_WIDTH), lambda b, t: (0, 0)),
                  tile(POOL_WIDTH), tile(ATTN_WIDTH), tile(ATTN_WIDTH), tile(D_MODEL), tile(D_MODEL)]
                 + _weight_specs(2),
        out_specs=tile(D_MODEL),
        out_shape=jax.ShapeDtypeStruct((bsz, seq, D_MODEL), f32),
        scratch_shapes=[pltpu.VMEM((N_META + TQ, POOL_WIDTH), f32)],
        compiler_params=_cparams(2),
    )(x, u, u, u_meta, gp, ao, ga, gtp, gta, *weights)


def _merge_sample_body(x_ref, u_ref, sp_ref, gp_ref, ao_ref, ga_ref, gtp_ref, gta_ref,
                       pw_ref, ps_ref, wbp_ref, wba_ref, wo_ref, lng_ref, lnb_ref, y_ref, *, tnew):
    gw = POOL_WIDTH // len(POOL_WINDOWS)

    def hist(t, cols):
        if t < POOL_BUF:
            return sp_ref[:, POOL_WIDTH * t + cols.start:POOL_WIDTH * t + cols.stop]
        tt = t - POOL_BUF
        return u_ref[:, POOL_WIDTH * tt + cols.start:POOL_WIDTH * tt + cols.stop]

    for i in range(tnew):
        parts = []
        for g, w in enumerate(POOL_WINDOWS):
            cols = slice(gw * g, gw * (g + 1))
            s = hist(POOL_BUF + i, cols)
            for j in range(1, w):
                s = s + hist(POOL_BUF + i - j, cols)
            parts.append(s * (1.0 / w) - hist(POOL_BUF + i, cols))
        d = jnp.concatenate(parts, axis=1)
        c5 = slice(POOL_WIDTH * i, POOL_WIDTH * (i + 1))
        c10 = slice(D_MODEL * i, D_MODEL * (i + 1))
        y_ref[:, c10] = _merge_core(x_ref[:, c10], d, gp_ref[:, c5], ao_ref[:, c5], ga_ref[:, c5],
                                    gtp_ref[:, c10], gta_ref[:, c10],
                                    pw_ref, ps_ref, wbp_ref, wba_ref, wo_ref, lng_ref, lnb_ref)


def _merge_sample(x, u, state_pool, gp, ao, ga, gtp, gta, weights, tnew):
    dbsz = x.shape[0]
    full = lambda a: pl.BlockSpec(a.shape, lambda i: (0, 0))
    args = (x, u, state_pool, gp, ao, ga, gtp, gta)
    body = functools.partial(_merge_sample_body, tnew=tnew)
    return pl.pallas_call(
        body,
        grid=(1,),
        in_specs=[full(a) for a in args] + _weight_specs(1),
        out_specs=pl.BlockSpec((dbsz, tnew * D_MODEL), lambda i: (0, 0)),
        out_shape=jax.ShapeDtypeStruct((dbsz, tnew * D_MODEL), f32),
        compiler_params=_cparams(1),
    )(*args, *weights)


def _rel_bucket(n):
    max_exact = N_BUCKETS // 2
    nf = jnp.maximum(n, 1).astype(f32)
    large = max_exact + (jnp.log(nf / max_exact) / math.log(MAX_DISTANCE / max_exact)
                         * (N_BUCKETS - max_exact)).astype(i32)
    large = jnp.minimum(large, N_BUCKETS - 1)
    return jnp.where(n < max_exact, n, large)


def _bias_table(rel_bias, dist):
    b = rel_bias[_rel_bucket(jnp.maximum(dist, 0))].astype(f32)
    return jnp.moveaxis(b, -1, 0)


def _pad_w_in(w_in):
    splits = (512, 512, 512, 512, 512, 512, 256, 64, 4, 1024, 1024)
    offs = np.cumsum((0,) + splits)
    part = {n: w_in[:, offs[i]:offs[i + 1]] for i, n in enumerate(
        ("u", "gp", "q", "k", "v", "ga", "qi", "ki", "wi", "gate_p", "gate_a"))}
    zpad = jnp.zeros((w_in.shape[0], LANES - N_IDX_HEADS), w_in.dtype)
    cols = [part["u"], part["gp"], part["q"] * (HEAD_DIM ** -0.5), part["k"], part["v"], part["ga"],
            part["qi"] * (IDX_DIM ** -0.5), part["ki"], part["ki"],
            part["wi"] * (N_IDX_HEADS ** -0.5), zpad, part["gate_p"], part["gate_a"]]
    w = jnp.concatenate(cols, axis=1)
    assert w.shape[1] == W_COLS
    return w.astype(bf16)


def kernel(x_prompt, x_sample, cache_k, cache_v, cache_kidx, state_pool, page_table, meta, w_in,
           pool_w, pool_scale, w_br_pool, w_br_attn, rel_bias, w_out, ln_g, ln_b):
    bsz, seq, _ = x_prompt.shape
    dbsz, tnew, _ = x_sample.shape
    n_phys, page = cache_k.shape[0], cache_k.shape[1]
    n_pages = page_table.shape[1]
    past_len = n_pages * page
    assert (HEAD_DIM ** -0.5, IDX_DIM ** -0.5, N_IDX_HEADS ** -0.5) == (0.125, 0.125, 0.5)
    assert tnew <= SROWS and N_META <= LANES and tnew <= POOL_BUF

    w_pad = _pad_w_in(w_in)
    weights = (pool_w.astype(bf16), pool_scale.reshape(1, POOL_WIDTH).astype(f32),
               w_br_pool.astype(bf16), w_br_attn.astype(bf16), w_out.astype(bf16),
               ln_g.reshape(1, D_MODEL).astype(f32), ln_b.reshape(1, D_MODEL).astype(f32))

    (u, gp, q, k, kb, v, vb, ga, qi, kidx, kib, wi, gtp, gta) = _project(
        x_prompt.reshape(bsz * seq, D_MODEL), w_pad, TQ)
    (u_m, _, _, k_m, kb_m, v_m, vb_m, _, _, kidx_m, kib_m, _, _, _) = _project(
        meta.astype(x_prompt.dtype), w_pad, N_META)
    r3 = lambda a: a.reshape(bsz, seq, a.shape[-1])
    pad_rows = lambda a: jnp.pad(a, ((0, LANES - N_META), (0, 0)))

    ar = jnp.arange(TQ)
    d_pp = jnp.stack([ar[:, None] - ar[None, :] + TQ * dd for dd in range(3)])
    d_pp = d_pp.at[2].set(MAX_DISTANCE)
    bias_pp = jnp.swapaxes(_bias_table(rel_bias, d_pp), 0, 1)
    d_m = ar[:, None] + N_META - jnp.arange(LANES)[None, :]
    d_m = jnp.stack([d_m, jnp.full_like(d_m, MAX_DISTANCE)])
    bias_meta = jnp.swapaxes(_bias_table(rel_bias, d_m), 0, 1)

    topk_p = min(TOPK_MAX, seq // 4)
    attn_o = _attn_prompt(r3(qi), r3(wi), r3(q), r3(kib), r3(kb), r3(vb),
                          pad_rows(kib_m), pad_rows(kb_m), pad_rows(vb_m), bias_pp, bias_meta, topk_p)
    y_prompt = _merge_prompt(x_prompt, r3(u), u_m, r3(gp), attn_o, r3(ga), r3(gtp), r3(gta), weights)

    def with_meta(m, a):
        mm = jnp.broadcast_to(m[None], (bsz,) + m.shape)
        return jnp.concatenate([mm, r3(a)], axis=1)

    k_prompt = with_meta(k_m, k).reshape(bsz, seq + N_META, N_HEADS, HEAD_DIM)
    v_prompt = with_meta(v_m, v).reshape(bsz, seq + N_META, N_HEADS, HEAD_DIM)
    kidx_prompt = with_meta(kidx_m, kidx)
    pool_prompt = r3(u)[:, seq - POOL_BUF:]

    (us, gps, qs, ks, kbs, vs, vbs, gas, qis, kidxs, kibs, wis, gtps, gtas) = _project(
        x_sample.reshape(dbsz * tnew, D_MODEL), w_pad, min(TQ, dbsz * tnew))
    s3 = lambda a: a.reshape(dbsz, tnew, a.shape[-1])
    padq = lambda a: jnp.pad(a, ((0, 0), (0, 0), (0, SROWS - tnew), (0, 0)))
    qi4 = jnp.swapaxes(s3(qis).reshape(dbsz, tnew, N_IDX_HEADS, IDX_DIM), 1, 2)
    qi32 = jnp.pad(padq(qi4), ((0, 0), (0, 0), (0, 0), (0, LANES - IDX_DIM)))
    qi32 = qi32.reshape(dbsz, N_IDX_HEADS * SROWS, LANES)
    wi4 = jnp.swapaxes(s3(wis)[:, :, :N_IDX_HEADS], 1, 2)[..., None]
    wb32 = jnp.broadcast_to(padq(wi4), (dbsz, N_IDX_HEADS, SROWS, LANES))
    wb32 = wb32.reshape(dbsz, N_IDX_HEADS * SROWS, LANES)
    kin2 = jnp.pad(s3(kibs), ((0, 0), (0, LANES - tnew), (0, 0)))
    scores_past, scores_new = _sample_scores(page_table, qi32, wb32, kin2, cache_kidx)

    topk_s = min(TOPK_MAX, (past_len + tnew) // 4)
    mask_past, mask_new = _sample_select(scores_past.reshape(dbsz * SROWS, past_len),
                                         scores_new.reshape(dbsz * SROWS, LANES), tnew, topk_s)

    head_of_lane = jnp.arange(ATTN_WIDTH) // HEAD_DIM
    q4 = jnp.where(head_of_lane[None, None, None, :] == jnp.arange(N_HEADS)[None, :, None, None],
                   s3(qs)[:, None, :, :], jnp.zeros((), bf16))
    q64 = padq(q4).reshape(dbsz, N_HEADS * SROWS, ATTN_WIDTH)
    pad8 = lambda a: jnp.pad(s3(a), ((0, 0), (0, KROWS - tnew), (0, 0)))
    qrow = jnp.arange(SROWS)
    d_s = (page + qrow)[:, None] - jnp.arange(LANES)[None, :]
    d_s = jnp.stack([jnp.full_like(d_s, MAX_DISTANCE), d_s])
    bias_s = jnp.swapaxes(_bias_table(rel_bias, d_s), 0, 1).reshape(2, N_HEADS * SROWS, LANES)
    d_n = qrow[:, None] - jnp.arange(LANES)[None, :]
    bias_n = _bias_table(rel_bias, d_n).reshape(N_HEADS * SROWS, LANES)
    ao_s = _sample_attend(page_table, q64,
                          mask_past.reshape(dbsz, SROWS, past_len), mask_new.reshape(dbsz, SROWS, LANES),
                          pad8(kbs), pad8(vbs), bias_s, bias_n,
                          cache_k.reshape(n_phys, page, ATTN_WIDTH), cache_v.reshape(n_phys, page, ATTN_WIDTH))
    ao_s = ao_s[:, :tnew].reshape(dbsz, tnew * ATTN_WIDTH)

    flat = lambda a: a.reshape(dbsz, -1)
    y_sample = _merge_sample(flat(x_sample), flat(us), flat(state_pool.astype(f32)), flat(gps), ao_s,
                             flat(gas), flat(gtps), flat(gtas), weights, tnew)
    y_sample = y_sample.reshape(dbsz, tnew, D_MODEL)

    k_sample = ks.reshape(dbsz, tnew, N_HEADS, HEAD_DIM)
    v_sample = vs.reshape(dbsz, tnew, N_HEADS, HEAD_DIM)
    kidx_sample = s3(kidxs)
    pool_sample = jnp.concatenate([state_pool.astype(f32), s3(us)], axis=1)[:, -POOL_BUF:]

    return (y_prompt, y_sample, k_prompt, v_prompt, kidx_prompt, pool_prompt,
            k_sample, v_sample, kidx_sample, pool_sample)
```

```python
import functools
import math

import numpy as np
import jax
import jax.numpy as jnp
from jax import lax
from jax.experimental import pallas as pl
from jax.experimental.pallas import tpu as pltpu

f32 = jnp.float32
bf16 = jnp.bfloat16
i32 = jnp.int32

N_META = 16
POOL_WINDOWS = (2, 4, 8, 16)
POOL_BUF = max(POOL_WINDOWS) - 1
N_HEADS = 8
HEAD_DIM = 64
N_IDX_HEADS = 4
IDX_DIM = 64
TOPK_MAX = 256
N_BUCKETS = 32
MAX_DISTANCE = 128
DEPTH = 1
ALPHA = (2 * DEPTH) ** 0.25
LN_EPS = 1e-5
MASKED_LOGIT = -1e30

LANES = 128
SUBLANES = 8
WORD_BITS = 32
VMEM_LIMIT = 56 * 1024 * 1024

TQ = 256
SUB_ROWS = 64
LOG2E = math.log2(math.e)
Q_SCALE = LOG2E * HEAD_DIM ** -0.5
GROUP_KEYS = WORD_BITS * LANES
INT_MIN = -(2 ** 31)

D_MODEL = 1024
POOL_WIDTH = 512
ATTN_WIDTH = N_HEADS * HEAD_DIM
PAIR = 2 * HEAD_DIM

SEG = {
    "u": (0, 512), "gp": (512, 512), "q": (1024, 512), "k": (1536, 512), "v": (2048, 512),
    "ga": (2560, 512), "qi": (3072, 256), "ki2": (3328, 128), "wi": (3456, 128),
    "gate_p": (3584, 1024), "gate_a": (4608, 1024),
}
W_COLS = 5632


def _cparams(n_axes):
    return pltpu.CompilerParams(dimension_semantics=("arbitrary",) * n_axes,
                                vmem_limit_bytes=VMEM_LIMIT)


def _project_body(x_ref, w_ref, u_o, gp_o, q_o, k_o, kb_o, v_o, vb_o, ga_o, qi_o, kidx_o, kib_o,
                  wi_o, gtp_o, gta_o):
    xb = x_ref[...].astype(bf16)

    def seg(name):
        off, n = SEG[name]
        return jnp.dot(xb, w_ref[:, off:off + n], preferred_element_type=f32)

    u_o[...] = seg("u")
    gp_o[...] = seg("gp")
    q_o[...] = (seg("q") * Q_SCALE).astype(bf16)
    k = seg("k")
    k_o[...] = k
    kb_o[...] = k.astype(bf16)
    v = seg("v")
    v_o[...] = v
    vb_o[...] = v.astype(bf16)
    ga_o[...] = seg("ga")
    qi_o[...] = seg("qi").astype(bf16)
    ki2 = seg("ki2")
    kidx_o[...] = ki2[:, :IDX_DIM]
    kib_o[...] = ki2.astype(bf16)
    wi_o[...] = seg("wi")
    gtp_o[...] = seg("gate_p")
    gta_o[...] = seg("gate_a")


def _project(x2d, w_pad, tm):
    rows = x2d.shape[0]
    assert rows % tm == 0
    widths = [(512, f32), (512, f32), (512, bf16), (512, f32), (512, bf16), (512, f32), (512, bf16),
              (512, f32), (256, bf16), (IDX_DIM, f32), (128, bf16), (128, f32), (1024, f32), (1024, f32)]
    out_shape = [jax.ShapeDtypeStruct((rows, n), dt) for n, dt in widths]
    out_specs = [pl.BlockSpec((tm, n), lambda i: (i, 0)) for n, _ in widths]
    return pl.pallas_call(
        _project_body,
        grid=(rows // tm,),
        in_specs=[pl.BlockSpec((tm, D_MODEL), lambda i: (i, 0)),
                  pl.BlockSpec((D_MODEL, W_COLS), lambda i: (0, 0))],
        out_specs=out_specs,
        out_shape=out_shape,
        compiler_params=_cparams(1),
    )(x2d, w_pad)


def _score_key(score):
    bits = lax.bitcast_convert_type(score, i32)
    sign = lax.shift_right_arithmetic(bits, jnp.full(bits.shape, 31, i32))
    mag = bits & jnp.int32(0x7FFFFFFF)
    return (mag ^ sign) - sign


def _bit_transpose32(words):
    a = list(words)
    j = 16
    m = 0x0000FFFF
    while j != 0:
        k = 0
        sh = jnp.full(a[0].shape, j, i32)
        mm = jnp.int32(np.uint32(m).astype(np.int32))
        while k < 32:
            t = (a[k] ^ lax.shift_right_logical(a[k + j], sh)) & mm
            a[k] = a[k] ^ t
            a[k + j] = a[k + j] ^ lax.shift_left(t, sh)
            k = (k + j + 1) & ~j
        j >>= 1
        if j:
            m = (m ^ (m << j)) & 0xFFFFFFFF
    return a


def _transpose_group(planes_scr, g, rows_total):
    def body(rg, carry):
        rows = pl.ds(pl.multiple_of(rg * SUBLANES, SUBLANES), SUBLANES)
        words = [planes_scr[g, WORD_BITS - 1 - c, rows, :] for c in range(WORD_BITS)]
        t = _bit_transpose32(words)
        for kk in range(WORD_BITS):
            b = WORD_BITS - 1 - kk
            planes_scr[g, b, rows, :] = ~t[kk] if b == WORD_BITS - 1 else t[kk]
        return carry

    lax.fori_loop(0, rows_total // SUBLANES, body, 0)


def _lane_total(cnt, ones_mat):
    return jnp.dot(cnt.astype(f32).astype(bf16), ones_mat, preferred_element_type=f32)


def _radix_select(planes_scr, single_key, alive_scr, sel_scr, krem_scr, n_groups, single_first,
                  alive_init, k_row):
    rows = single_key.shape[0]
    nset = n_groups + 1
    ones_mat = jnp.ones((LANES, LANES), bf16)
    zero = jnp.zeros((rows, LANES), i32)

    for s in range(nset):
        alive_scr[s] = alive_init[s]
        sel_scr[s] = zero
    krem_scr[...] = k_row

    def step(planes):
        alive = [alive_scr[s] for s in range(nset)]
        ones = [alive[s] & planes[s] for s in range(nset)]
        cnt = ones[0]
        for s in range(1, nset):
            cnt = cnt + lax.population_count(ones[s])
        tot = _lane_total(cnt, ones_mat)
        krem = krem_scr[...]
        take = tot >= krem
        for s in range(nset):
            alive_scr[s] = jnp.where(take, ones[s], alive[s] ^ ones[s])
            sel_scr[s] = sel_scr[s] | jnp.where(take, zero, ones[s])
        krem_scr[...] = jnp.where(take, krem, krem - tot)

    def value_step(t, carry):
        b = WORD_BITS - 1 - t
        sh = jnp.full((rows, LANES), b, i32)
        planes = [lax.shift_right_logical(single_key, sh) & 1]
        for g in range(n_groups):
            planes.append(planes_scr[g, b])
        step(planes)
        return carry

    lax.fori_loop(0, WORD_BITS, value_step, 0)

    full = jnp.full((rows, LANES), -1, i32)
    one = jnp.full((rows, LANES), 1, i32)
    codes = [0] + [g + 1 for g in range(n_groups)] if single_first else [n_groups] + list(range(n_groups))
    n_code_bits = max(1, (max(codes)).bit_length())
    for cb in reversed(range(n_code_bits)):
        planes = []
        for s in range(nset):
            pref = ((codes[s] >> cb) & 1) == 0
            planes.append((one if s == 0 else full) if pref else zero)
        step(planes)
    chunk_masks = (0x0000FFFF, 0x00FF00FF, 0x0F0F0F0F, 0x33333333, 0x55555555)
    for cm in chunk_masks:
        word = jnp.full((rows, LANES), np.uint32(cm).astype(np.int32), i32)
        step([one] + [word] * n_groups)
    lane = lax.broadcasted_iota(i32, (rows, LANES), 1)
    for lb in reversed(range(7)):
        low = ((lane >> lb) & 1) == 0
        lw = jnp.where(low, full, zero)
        step([lw & 1] + [lw] * n_groups)
    for s in range(nset):
        sel_scr[s] = sel_scr[s] | alive_scr[s]


def _mask_from_bits(word, bit):
    sh = jnp.full(word.shape, bit, i32)
    picked = (lax.shift_right_logical(word, sh) & 1) != 0
    return jnp.where(picked, 0.0, MASKED_LOGIT).astype(f32)


def _half_masks(x_pair, lane_lo):
    xf = x_pair.astype(f32)
    z = jnp.zeros_like(xf)
    return (jnp.where(lane_lo, xf, z).astype(x_pair.dtype),
            jnp.where(lane_lo, z, xf).astype(x_pair.dtype))


def _attn_prompt_body(qi_ref, wi_ref, q_ref, ki_ref, k_ref, v_ref, kim_ref, km_ref, vm_ref,
                      bpp_ref, bmeta_ref, o_ref,
                      planes_scr, alive_scr, sel_scr, krem_scr, m_scr, l_scr, acc_scr,
                      qcat_scr, s_scr, p_scr, vl_scr, madd_scr, alpha_scr,
                      *, n_groups, topk):
    i = pl.program_id(1)
    lane = lax.broadcasted_iota(i32, (TQ, LANES), 1)
    lane_lo = lane < HEAD_DIM
    lane_lo_sub = lax.broadcasted_iota(i32, (SUB_ROWS, LANES), 1) < HEAD_DIM
    NT = (((1,), (1,)), ((), ()))

    qi = qi_ref[0]
    qim = []
    for hp in range(N_IDX_HEADS // 2):
        qim.extend(_half_masks(qi[:, PAIR * hp:PAIR * (hp + 1)], lane_lo))
    wi = wi_ref[0]
    wcol = [wi[:, h:h + 1] for h in range(N_IDX_HEADS)]

    def score_block(kt2):
        acc = None
        for h in range(N_IDX_HEADS):
            s = lax.dot_general(qim[h], kt2, NT, preferred_element_type=f32)
            t = jnp.maximum(s, 0.0) * wcol[h]
            acc = t if acc is None else acc + t
        return acc

    def store_keys(j, key):
        c0 = 2 * j
        g = c0 // WORD_BITS
        jj = c0 % WORD_BITS
        planes_scr[g, jj] = key[:, :LANES]
        planes_scr[g, jj + 1] = key[:, LANES:]

    def score_tile(j, carry):
        kt2 = ki_ref[0, pl.ds(pl.multiple_of(j * TQ, TQ), TQ), :]
        store_keys(j, _score_key(score_block(kt2)))
        return carry

    lax.fori_loop(0, i, score_tile, 0)
    row2 = lax.broadcasted_iota(i32, (TQ, TQ), 0)
    col2 = lax.broadcasted_iota(i32, (TQ, TQ), 1)
    kt2 = ki_ref[0, pl.ds(pl.multiple_of(i * TQ, TQ), TQ), :]
    key_d = jnp.where(col2 <= row2, _score_key(score_block(kt2)), INT_MIN)
    store_keys(i, key_d)

    n_chunks = 2 * (i + 1)
    groups_used = (n_chunks + WORD_BITS - 1) // WORD_BITS
    fill = jnp.full((TQ, LANES), INT_MIN, i32)

    def fill_chunk(c, carry):
        planes_scr[c // WORD_BITS, c % WORD_BITS] = fill
        return carry

    lax.fori_loop(n_chunks, groups_used * WORD_BITS, fill_chunk, 0)

    key_m = _score_key(score_block(kim_ref[...]))
    key_m = jnp.where(lane < N_META, key_m, INT_MIN) ^ jnp.int32(INT_MIN)

    for g in range(n_groups):
        @pl.when(g < groups_used)
        def _():
            _transpose_group(planes_scr, g, TQ)

    full = jnp.full((TQ, LANES), -1, i32)
    zero = jnp.zeros((TQ, LANES), i32)
    alive_init = [jnp.full((TQ, LANES), 1, i32)]
    for g in range(n_groups):
        alive_init.append(jnp.where(g < groups_used, full, zero))
    row = lax.broadcasted_iota(i32, (TQ, LANES), 0)
    n_causal = i * TQ + row + (N_META + 1)
    k_row = jnp.minimum(n_causal, topk).astype(f32)
    _radix_select(planes_scr, key_m, alive_scr, sel_scr, krem_scr, n_groups, True, alive_init, k_row)

    q = q_ref[0]
    for hp in range(N_HEADS // 2):
        qe, qo = _half_masks(q[:, PAIR * hp:PAIR * (hp + 1)], lane_lo)
        qcat_scr[hp, 0:TQ] = qe
        qcat_scr[hp, TQ:2 * TQ] = qo
    for h in range(N_HEADS):
        m_scr[h] = jnp.full((TQ, LANES), -jnp.inf, f32)
    for hp in range(N_HEADS // 2):
        l_scr[hp] = jnp.zeros((TQ, LANES), f32)
        acc_scr[hp] = jnp.zeros((TQ, LANES), f32)

    lane_tq = lax.broadcasted_iota(i32, (TQ, LANES), 1) < HEAD_DIM
    for hp in range(N_HEADS // 2):
        vl_scr[hp, 0:TQ, PAIR:2 * PAIR] = jnp.where(lane_tq, 1.0, 0.0).astype(bf16)
        vl_scr[hp, TQ:2 * TQ, PAIR:2 * PAIR] = jnp.where(lane_tq, 0.0, 1.0).astype(bf16)

    def attend(width, kslab, vslab, bias_fn):
        lane_w = lax.broadcasted_iota(i32, (width, LANES), 1) < HEAD_DIM
        for hp in range(N_HEADS // 2):
            s_scr[hp, :, 0:width] = lax.dot_general(qcat_scr[hp], kslab(hp), NT,
                                                    preferred_element_type=f32)
            vf = vslab(hp).astype(f32)
            vl_scr[hp, 0:width, 0:PAIR] = jnp.where(lane_w, vf, 0.0).astype(bf16)
            vl_scr[hp, TQ:TQ + width, 0:PAIR] = jnp.where(lane_w, 0.0, vf).astype(bf16)
            for e in range(2):
                h = 2 * hp + e
                for r0 in range(0, TQ, SUB_ROWS):
                    rows = slice(r0, r0 + SUB_ROWS)
                    s = (s_scr[hp, e * TQ + r0:e * TQ + r0 + SUB_ROWS, 0:width]
                         + madd_scr[rows, 0:width])
                    if bias_fn is not None:
                        s = s + bias_fn(h, rows)
                    m_old = m_scr[h, rows]
                    m_new = jnp.maximum(m_old, jnp.max(s, axis=-1, keepdims=True))
                    alpha = jnp.exp2(m_old - m_new)
                    m_scr[h, rows] = m_new
                    m_rep = m_new if width == LANES else jnp.concatenate([m_new] * (width // LANES), axis=1)
                    p_scr[hp, rows, e * TQ:e * TQ + width] = jnp.exp2(s - m_rep).astype(bf16)
                    if e == 0:
                        alpha_scr[hp, rows] = alpha
                    else:
                        alpha_scr[hp, rows] = jnp.where(lane_lo_sub, alpha_scr[hp, rows], alpha)
            if width == TQ:
                ol = jnp.dot(p_scr[hp], vl_scr[hp], preferred_element_type=f32)
            else:
                ol = (jnp.dot(p_scr[hp, :, 0:width], vl_scr[hp, 0:width], preferred_element_type=f32)
                      + jnp.dot(p_scr[hp, :, TQ:TQ + width], vl_scr[hp, TQ:TQ + width],
                                preferred_element_type=f32))
            a = alpha_scr[hp]
            acc_scr[hp] = a * acc_scr[hp] + ol[:, 0:PAIR]
            l_scr[hp] = a * l_scr[hp] + ol[:, PAIR:2 * PAIR]

    def pair(ref, rows):
        return lambda hp: ref[0, rows, PAIR * hp:PAIR * (hp + 1)]

    madd_scr[:, 0:LANES] = _mask_from_bits(sel_scr[0], 0)
    meta_k = lambda hp: km_ref[:, PAIR * hp:PAIR * (hp + 1)]
    meta_v = lambda hp: vm_ref[:, PAIR * hp:PAIR * (hp + 1)]

    @pl.when(i == 0)
    def _():
        attend(LANES, meta_k, meta_v, lambda h, rows: bmeta_ref[h, rows, :])

    @pl.when(i > 0)
    def _():
        attend(LANES, meta_k, meta_v, None)

    def set_mask(j):
        c0 = 2 * j
        word = sel_scr[1 + c0 // WORD_BITS]
        jj = c0 % WORD_BITS
        madd_scr[:, 0:LANES] = _mask_from_bits(word, jj)
        madd_scr[:, LANES:2 * LANES] = _mask_from_bits(word, jj + 1)

    def key_rows(j):
        return pl.ds(pl.multiple_of(j * TQ, TQ), TQ)

    def far_tile(j, carry):
        set_mask(j)
        attend(TQ, pair(k_ref, key_rows(j)), pair(v_ref, key_rows(j)), None)
        return carry

    lax.fori_loop(0, i - 1, far_tile, 0)

    @pl.when(i > 0)
    def _():
        set_mask(i - 1)
        attend(TQ, pair(k_ref, key_rows(i - 1)), pair(v_ref, key_rows(i - 1)),
               lambda h, rows: bpp_ref[1, h, rows, :])

    set_mask(i)
    attend(TQ, pair(k_ref, key_rows(i)), pair(v_ref, key_rows(i)),
           lambda h, rows: bpp_ref[0, h, rows, :])

    for hp in range(N_HEADS // 2):
        o_ref[0, :, PAIR * hp:PAIR * (hp + 1)] = acc_scr[hp] / l_scr[hp]


def _attn_prompt(qi, wi, q, kib, kb, vb, kim, km, vm, bias_pp, bias_meta, topk):
    bsz, seq, _ = q.shape
    assert seq % TQ == 0
    n_groups = -(-seq // GROUP_KEYS)
    nq = seq // TQ
    const = lambda nd: (lambda b, i: (0,) * nd)
    body = functools.partial(_attn_prompt_body, n_groups=n_groups, topk=topk)
    return pl.pallas_call(
        body,
        grid=(bsz, nq),
        in_specs=[
            pl.BlockSpec((1, TQ, N_IDX_HEADS * IDX_DIM), lambda b, i: (b, i, 0)),
            pl.BlockSpec((1, TQ, LANES), lambda b, i: (b, i, 0)),
            pl.BlockSpec((1, TQ, ATTN_WIDTH), lambda b, i: (b, i, 0)),
            pl.BlockSpec((1, seq, LANES), lambda b, i: (b, 0, 0), pipeline_mode=pl.Buffered(1)),
            pl.BlockSpec((1, seq, ATTN_WIDTH), lambda b, i: (b, 0, 0), pipeline_mode=pl.Buffered(1)),
            pl.BlockSpec((1, seq, ATTN_WIDTH), lambda b, i: (b, 0, 0), pipeline_mode=pl.Buffered(1)),
            pl.BlockSpec((LANES, LANES), const(2)),
            pl.BlockSpec((LANES, ATTN_WIDTH), const(2)),
            pl.BlockSpec((LANES, ATTN_WIDTH), const(2)),
            pl.BlockSpec((2, N_HEADS, TQ, TQ), const(4), pipeline_mode=pl.Buffered(1)),
            pl.BlockSpec((N_HEADS, TQ, LANES), const(3), pipeline_mode=pl.Buffered(1)),
        ],
        out_specs=pl.BlockSpec((1, TQ, ATTN_WIDTH), lambda b, i: (b, i, 0)),
        out_shape=jax.ShapeDtypeStruct((bsz, seq, ATTN_WIDTH), f32),
        scratch_shapes=[
            pltpu.VMEM((n_groups, WORD_BITS, TQ, LANES), i32),
            pltpu.VMEM((n_groups + 1, TQ, LANES), i32),
            pltpu.VMEM((n_groups + 1, TQ, LANES), i32),
            pltpu.VMEM((TQ, LANES), f32),
            pltpu.VMEM((N_HEADS, TQ, LANES), f32),
            pltpu.VMEM((N_HEADS // 2, TQ, LANES), f32),
            pltpu.VMEM((N_HEADS // 2, TQ, LANES), f32),
            pltpu.VMEM((N_HEADS // 2, 2 * TQ, PAIR), bf16),
            pltpu.VMEM((N_HEADS // 2, 2 * TQ, TQ), f32),
            pltpu.VMEM((N_HEADS // 2, TQ, 2 * TQ), bf16),
            pltpu.VMEM((N_HEADS // 2, 2 * TQ, 2 * PAIR), bf16),
            pltpu.VMEM((TQ, TQ), f32),
            pltpu.VMEM((N_HEADS // 2, TQ, LANES), f32),
        ],
        compiler_params=_cparams(2),
    )(qi, wi, q, kib, kb, vb, kim, km, vm, bias_pp, bias_meta)


PAGES_PER_STEP = 8
IDX_PAGES_PER_STEP = 16
SROWS = 8
SAMPLE_SUB = 16


def _sample_scores_body(pt_ref, qi_ref, wb_ref, kin_ref, *rest):
    page_refs = rest[:IDX_PAGES_PER_STEP]
    out_ref, outn_ref = rest[IDX_PAGES_PER_STEP:]
    NT = (((1,), (1,)), ((), ()))
    qi = qi_ref[0]
    wb = wb_ref[0]

    def combine(s):
        t = jnp.maximum(s, 0.0) * wb
        acc = t[0:SROWS]
        for h in range(1, N_IDX_HEADS):
            acc = acc + t[SROWS * h:SROWS * (h + 1)]
        return acc

    zpad = jnp.zeros((LANES - IDX_DIM, LANES), bf16)
    for jx in range(IDX_PAGES_PER_STEP):
        kt = jnp.concatenate([page_refs[jx][0].astype(bf16), zpad], axis=0)
        out_ref[0, :, LANES * jx:LANES * (jx + 1)] = combine(
            jnp.dot(qi, kt, preferred_element_type=f32))
    outn_ref[0] = combine(lax.dot_general(qi, kin_ref[0], NT, preferred_element_type=f32))


def _sample_scores(page_table, qi32, wb32, kin2, kidx_t):
    dbsz, n_pages = page_table.shape
    page = kidx_t.shape[2]
    assert page == LANES and n_pages % IDX_PAGES_PER_STEP == 0
    nsteps = n_pages // IDX_PAGES_PER_STEP
    mrows = N_IDX_HEADS * SROWS

    def page_spec(jx):
        return pl.BlockSpec((1, IDX_DIM, page),
                            lambda b, s, pt: (pt[b, s * IDX_PAGES_PER_STEP + jx], 0, 0))

    grid_spec = pltpu.PrefetchScalarGridSpec(
        num_scalar_prefetch=1,
        grid=(dbsz, nsteps),
        in_specs=[pl.BlockSpec((1, mrows, LANES), lambda b, s, pt: (b, 0, 0)),
                  pl.BlockSpec((1, mrows, LANES), lambda b, s, pt: (b, 0, 0)),
                  pl.BlockSpec((1, LANES, LANES), lambda b, s, pt: (b, 0, 0))]
                 + [page_spec(jx) for jx in range(IDX_PAGES_PER_STEP)],
        out_specs=[pl.BlockSpec((1, SROWS, IDX_PAGES_PER_STEP * LANES), lambda b, s, pt: (b, 0, s)),
                   pl.BlockSpec((1, SROWS, LANES), lambda b, s, pt: (b, 0, 0))],
    )
    return pl.pallas_call(
        _sample_scores_body,
        grid_spec=grid_spec,
        out_shape=[jax.ShapeDtypeStruct((dbsz, SROWS, n_pages * page), f32),
                   jax.ShapeDtypeStruct((dbsz, SROWS, LANES), f32)],
        compiler_params=_cparams(2),
    )(page_table, qi32, wb32, kin2, *([kidx_t] * IDX_PAGES_PER_STEP))


SEL_ROWS = 128


def _sample_select_body(sp_ref, sn_ref, mp_ref, mn_ref, planes_scr, alive_scr, sel_scr, krem_scr,
                        *, n_groups, n_new, topk):
    n_chunks = sp_ref.shape[1] // LANES

    def key_chunk(c, carry):
        col = pl.ds(pl.multiple_of(c * LANES, LANES), LANES)
        planes_scr[c // WORD_BITS, c % WORD_BITS] = _score_key(sp_ref[:, col])
        return carry

    lax.fori_loop(0, n_chunks, key_chunk, 0)
    for c in range(n_chunks, n_groups * WORD_BITS):
        planes_scr[c // WORD_BITS, c % WORD_BITS] = jnp.full((SEL_ROWS, LANES), INT_MIN, i32)
    lane = lax.broadcasted_iota(i32, (SEL_ROWS, LANES), 1)
    row = lax.broadcasted_iota(i32, (SEL_ROWS, LANES), 0)
    causal_new = (lane <= (row & (SROWS - 1))) & (lane < n_new)
    key_n = jnp.where(causal_new, _score_key(sn_ref[...]), INT_MIN) ^ jnp.int32(INT_MIN)

    for g in range(n_groups):
        _transpose_group(planes_scr, g, SEL_ROWS)
    full = jnp.full((SEL_ROWS, LANES), -1, i32)
    alive_init = [jnp.full((SEL_ROWS, LANES), 1, i32)] + [full] * n_groups
    k_row = jnp.full((SEL_ROWS, LANES), topk, f32)
    _radix_select(planes_scr, key_n, alive_scr, sel_scr, krem_scr, n_groups, False, alive_init, k_row)

    def mask_chunk(c, carry):
        col = pl.ds(pl.multiple_of(c * LANES, LANES), LANES)
        mp_ref[:, col] = _mask_from_bits(sel_scr[1 + c // WORD_BITS], c % WORD_BITS)
        return carry

    lax.fori_loop(0, n_chunks, mask_chunk, 0)
    mn_ref[...] = _mask_from_bits(sel_scr[0], 0)


def _sample_select(scores_past, scores_new, n_new, topk):
    rows, past = scores_past.shape
    assert rows % SEL_ROWS == 0 and past % LANES == 0
    n_groups = -(-past // GROUP_KEYS)
    body = functools.partial(_sample_select_body, n_groups=n_groups, n_new=n_new, topk=topk)
    return pl.pallas_call(
        body,
        grid=(rows // SEL_ROWS,),
        in_specs=[pl.BlockSpec((SEL_ROWS, past), lambda r: (r, 0)),
                  pl.BlockSpec((SEL_ROWS, LANES), lambda r: (r, 0))],
        out_specs=[pl.BlockSpec((SEL_ROWS, past), lambda r: (r, 0)),
                   pl.BlockSpec((SEL_ROWS, LANES), lambda r: (r, 0))],
        out_shape=[jax.ShapeDtypeStruct((rows, past), f32),
                   jax.ShapeDtypeStruct((rows, LANES), f32)],
        scratch_shapes=[
            pltpu.VMEM((n_groups, WORD_BITS, SEL_ROWS, LANES), i32),
            pltpu.VMEM((n_groups + 1, SEL_ROWS, LANES), i32),
            pltpu.VMEM((n_groups + 1, SEL_ROWS, LANES), i32),
            pltpu.VMEM((SEL_ROWS, LANES), f32),
        ],
        compiler_params=_cparams(1),
    )(scores_past, scores_new)


def _sample_attend_body(pt_ref, q_ref, mp_ref, mn_ref, knt_ref, vnt_ref, bs_ref, bn_ref, *rest):
    k_refs = rest[:PAGES_PER_STEP]
    v_refs = rest[PAGES_PER_STEP:2 * PAGES_PER_STEP]
    o_ref, m_scr, l_scr, acc_scr, kcat_scr, vcat_scr, s_scr, p_scr, alpha_scr = rest[2 * PAGES_PER_STEP:]
    s_idx = pl.program_id(1)
    last_step = s_idx == pl.num_programs(1) - 1
    NT = (((1,), (1,)), ((), ()))
    mrows = N_HEADS * SROWS
    q = q_ref[0]

    @pl.when(s_idx == 0)
    def _():
        m_scr[...] = jnp.full((mrows, LANES), -jnp.inf, f32)
        l_scr[...] = jnp.zeros((mrows, LANES), f32)
        acc_scr[...] = jnp.zeros((mrows, ATTN_WIDTH), f32)

    def attend(width, bias_tail, maskadd):
        s_scr[:, 0:width] = jnp.dot(q, kcat_scr[:, 0:width], preferred_element_type=f32)
        mk = jnp.concatenate([maskadd] * (SAMPLE_SUB // SROWS), axis=0)
        for r0 in range(0, mrows, SAMPLE_SUB):
            rows = slice(r0, r0 + SAMPLE_SUB)
            s = s_scr[rows, 0:width] + mk
            tail = s[:, width - LANES:] + bias_tail[rows]
            s = tail if width == LANES else jnp.concatenate([s[:, :width - LANES], tail], axis=1)
            m_old = m_scr[rows]
            m_new = jnp.maximum(m_old, jnp.max(s, axis=-1, keepdims=True))
            alpha = jnp.exp2(m_old - m_new)
            m_scr[rows] = m_new
            m_rep = m_new if width == LANES else jnp.concatenate([m_new] * (width // LANES), axis=1)
            p = jnp.exp2(s - m_rep)
            l_scr[rows] = alpha * l_scr[rows] + jnp.sum(p, axis=-1, keepdims=True)
            p_scr[rows, 0:width] = p.astype(bf16)
            alpha_scr[rows] = alpha
        o = lax.dot_general(p_scr[:, 0:width], vcat_scr[:, 0:width], NT, preferred_element_type=f32)
        aw = jnp.concatenate([alpha_scr[...]] * (ATTN_WIDTH // LANES), axis=1)
        acc_scr[...] = aw * acc_scr[...] + o

    for jx in range(PAGES_PER_STEP):
        kcat_scr[:, LANES * jx:LANES * (jx + 1)] = k_refs[jx][0].astype(bf16)
        vcat_scr[:, LANES * jx:LANES * (jx + 1)] = v_refs[jx][0].astype(bf16)
    attend(PAGES_PER_STEP * LANES, bs_ref[...] * jnp.where(last_step, 1.0, 0.0), mp_ref[0])

    @pl.when(last_step)
    def _():
        kcat_scr[:, 0:LANES] = knt_ref[0]
        vcat_scr[:, 0:LANES] = vnt_ref[0]
        attend(LANES, bn_ref[...], mn_ref[0])
        l_w = jnp.concatenate([l_scr[...]] * (ATTN_WIDTH // LANES), axis=1)
        accn = acc_scr[...] / l_w
        head_of_lane = lax.broadcasted_iota(i32, (SROWS, ATTN_WIDTH), 1) // HEAD_DIM
        out = jnp.zeros((SROWS, ATTN_WIDTH), f32)
        for h in range(N_HEADS):
            out = out + jnp.where(head_of_lane == h, accn[SROWS * h:SROWS * (h + 1)], 0.0)
        o_ref[0] = out


def _sample_attend(page_table, q64, mask_past, mask_new, knt, vnt, bias_s, bias_n, k_t, v_t):
    dbsz, n_pages = page_table.shape
    page = k_t.shape[2]
    assert page == LANES and n_pages % PAGES_PER_STEP == 0
    nsteps = n_pages // PAGES_PER_STEP
    mrows = N_HEADS * SROWS
    width = PAGES_PER_STEP * LANES

    def page_spec(jx):
        return pl.BlockSpec((1, ATTN_WIDTH, page), lambda b, s, pt: (pt[b, s * PAGES_PER_STEP + jx], 0, 0))

    grid_spec = pltpu.PrefetchScalarGridSpec(
        num_scalar_prefetch=1,
        grid=(dbsz, nsteps),
        in_specs=[pl.BlockSpec((1, mrows, ATTN_WIDTH), lambda b, s, pt: (b, 0, 0)),
                  pl.BlockSpec((1, SROWS, width), lambda b, s, pt: (b, 0, s)),
                  pl.BlockSpec((1, SROWS, LANES), lambda b, s, pt: (b, 0, 0)),
                  pl.BlockSpec((1, ATTN_WIDTH, LANES), lambda b, s, pt: (b, 0, 0)),
                  pl.BlockSpec((1, ATTN_WIDTH, LANES), lambda b, s, pt: (b, 0, 0)),
                  pl.BlockSpec((mrows, LANES), lambda b, s, pt: (0, 0)),
                  pl.BlockSpec((mrows, LANES), lambda b, s, pt: (0, 0))]
                 + [page_spec(jx) for jx in range(PAGES_PER_STEP)]
                 + [page_spec(jx) for jx in range(PAGES_PER_STEP)],
        out_specs=pl.BlockSpec((1, SROWS, ATTN_WIDTH), lambda b, s, pt: (b, 0, 0)),
        scratch_shapes=[pltpu.VMEM((mrows, LANES), f32),
                        pltpu.VMEM((mrows, LANES), f32),
                        pltpu.VMEM((mrows, ATTN_WIDTH), f32),
                        pltpu.VMEM((ATTN_WIDTH, width), bf16),
                        pltpu.VMEM((ATTN_WIDTH, width), bf16),
                        pltpu.VMEM((mrows, width), f32),
                        pltpu.VMEM((mrows, width), bf16),
                        pltpu.VMEM((mrows, LANES), f32)],
    )
    return pl.pallas_call(
        _sample_attend_body,
        grid_spec=grid_spec,
        out_shape=jax.ShapeDtypeStruct((dbsz, SROWS, ATTN_WIDTH), f32),
        compiler_params=_cparams(2),
    )(page_table, q64, mask_past, mask_new, knt, vnt, bias_s, bias_n,
      *([k_t] * PAGES_PER_STEP), *([v_t] * PAGES_PER_STEP))


def _silu(x):
    return x * jax.nn.sigmoid(x)


def _merge_core(x, d, gp, ao, ga, gtp, gta, pw_ref, ps_ref, wbp_ref, wba_ref, wo_ref, lng_ref, lnb_ref):
    gw = POOL_WIDTH // len(POOL_WINDOWS)
    parts = [jnp.dot(d[:, gw * g:gw * (g + 1)].astype(bf16), pw_ref[g], preferred_element_type=f32)
             for g in range(len(POOL_WINDOWS))]
    pool_o = jnp.concatenate(parts, axis=1) * ps_ref[...]
    bp = jnp.dot((pool_o * _silu(gp)).astype(bf16), wbp_ref[...], preferred_element_type=f32)
    ba = jnp.dot((ao * _silu(ga)).astype(bf16), wba_ref[...], preferred_element_type=f32)
    m = jax.nn.sigmoid(gtp) * bp + jax.nn.sigmoid(gta) * ba
    out = jnp.dot(m.astype(bf16), wo_ref[...], preferred_element_type=f32)
    z = ALPHA * x + out
    mu = jnp.mean(z, axis=-1, keepdims=True)
    zc = z - mu
    var = jnp.mean(zc * zc, axis=-1, keepdims=True)
    return zc * lax.rsqrt(var + LN_EPS) * lng_ref[...] + lnb_ref[...]


def _merge_prompt_body(x_ref, u_ref, uh_ref, um_ref, gp_ref, ao_ref, ga_ref, gtp_ref, gta_ref,
                       pw_ref, ps_ref, wbp_ref, wba_ref, wo_ref, lng_ref, lnb_ref, y_ref, uu_scr):
    t = pl.program_id(1)
    halo = N_META
    uu_scr[0:halo, :] = jnp.where(t == 0, um_ref[...], uh_ref[0])
    uu_scr[halo:halo + TQ, :] = u_ref[0]
    gw = POOL_WIDTH // len(POOL_WINDOWS)
    parts = []
    for g, w in enumerate(POOL_WINDOWS):
        cols = slice(gw * g, gw * (g + 1))
        s = uu_scr[halo:halo + TQ, cols]
        for j in range(1, w):
            s = s + uu_scr[halo - j:halo - j + TQ, cols]
        parts.append(s * (1.0 / w) - uu_scr[halo:halo + TQ, cols])
    d = jnp.concatenate(parts, axis=1)
    y_ref[0] = _merge_core(x_ref[0], d, gp_ref[0], ao_ref[0], ga_ref[0], gtp_ref[0], gta_ref[0],
                           pw_ref, ps_ref, wbp_ref, wba_ref, wo_ref, lng_ref, lnb_ref)


def _weight_specs(nd_grid):
    z = lambda nd: (lambda *a: (0,) * nd)
    gw = POOL_WIDTH // len(POOL_WINDOWS)
    return [pl.BlockSpec((len(POOL_WINDOWS), gw, gw), z(3)),
            pl.BlockSpec((1, POOL_WIDTH), z(2)),
            pl.BlockSpec((POOL_WIDTH, D_MODEL), z(2)),
            pl.BlockSpec((ATTN_WIDTH, D_MODEL), z(2)),
            pl.BlockSpec((D_MODEL, D_MODEL), z(2)),
            pl.BlockSpec((1, D_MODEL), z(2)),
            pl.BlockSpec((1, D_MODEL), z(2))]


def _merge_prompt(x, u, u_meta, gp, ao, ga, gtp, gta, weights):
    bsz, seq, _ = x.shape
    nt = seq // TQ
    hb = TQ // N_META
    tile = lambda n: pl.BlockSpec((1, TQ, n), lambda b, t: (b, t, 0))
    return pl.pallas_call(
        _merge_prompt_body,
        grid=(bsz, nt),
        in_specs=[tile(D_MODEL), tile(POOL_WIDTH),
                  pl.BlockSpec((1, N_META, POOL_WIDTH), lambda b, t: (b, jnp.maximum(t * hb - 1, 0), 0)),
                  pl.BlockSpec((N_META, POOL_WIDTH), lambda b, t: (0, 0)),
                  tile(POOL_WIDTH), tile(ATTN_WIDTH), tile(ATTN_WIDTH), tile(D_MODEL), tile(D_MODEL)]
                 + _weight_specs(2),
        out_specs=tile(D_MODEL),
        out_shape=jax.ShapeDtypeStruct((bsz, seq, D_MODEL), f32),
        scratch_shapes=[pltpu.VMEM((N_META + TQ, POOL_WIDTH), f32)],
        compiler_params=_cparams(2),
    )(x, u, u, u_meta, gp, ao, ga, gtp, gta, *weights)


def _merge_sample_body(x_ref, u_ref, sp_ref, gp_ref, ao_ref, ga_ref, gtp_ref, gta_ref,
                       pw_ref, ps_ref, wbp_ref, wba_ref, wo_ref, lng_ref, lnb_ref, y_ref, *, tnew):
    gw = POOL_WIDTH // len(POOL_WINDOWS)

    def hist(t, cols):
        if t < POOL_BUF:
            return sp_ref[:, POOL_WIDTH * t + cols.start:POOL_WIDTH * t + cols.stop]
        tt = t - POOL_BUF
        return u_ref[:, POOL_WIDTH * tt + cols.start:POOL_WIDTH * tt + cols.stop]

    for i in range(tnew):
        parts = []
        for g, w in enumerate(POOL_WINDOWS):
            cols = slice(gw * g, gw * (g + 1))
            s = hist(POOL_BUF + i, cols)
            for j in range(1, w):
                s = s + hist(POOL_BUF + i - j, cols)
            parts.append(s * (1.0 / w) - hist(POOL_BUF + i, cols))
        d = jnp.concatenate(parts, axis=1)
        c5 = slice(POOL_WIDTH * i, POOL_WIDTH * (i + 1))
        c10 = slice(D_MODEL * i, D_MODEL * (i + 1))
        y_ref[:, c10] = _merge_core(x_ref[:, c10], d, gp_ref[:, c5], ao_ref[:, c5], ga_ref[:, c5],
                                    gtp_ref[:, c10], gta_ref[:, c10],
                                    pw_ref, ps_ref, wbp_ref, wba_ref, wo_ref, lng_ref, lnb_ref)


def _merge_sample(x, u, state_pool, gp, ao, ga, gtp, gta, weights, tnew):
    dbsz = x.shape[0]
    full = lambda a: pl.BlockSpec(a.shape, lambda i: (0, 0))
    args = (x, u, state_pool, gp, ao, ga, gtp, gta)
    body = functools.partial(_merge_sample_body, tnew=tnew)
    return pl.pallas_call(
        body,
        grid=(1,),
        in_specs=[full(a) for a in args] + _weight_specs(1),
        out_specs=pl.BlockSpec((dbsz, tnew * D_MODEL), lambda i: (0, 0)),
        out_shape=jax.ShapeDtypeStruct((dbsz, tnew * D_MODEL), f32),
        compiler_params=_cparams(1),
    )(*args, *weights)


def _rel_bucket(n):
    max_exact = N_BUCKETS // 2
    nf = jnp.maximum(n, 1).astype(f32)
    large = max_exact + (jnp.log(nf / max_exact) / math.log(MAX_DISTANCE / max_exact)
                         * (N_BUCKETS - max_exact)).astype(i32)
    large = jnp.minimum(large, N_BUCKETS - 1)
    return jnp.where(n < max_exact, n, large)


def _bias_table(rel_bias, dist):
    bucket = _rel_bucket(jnp.maximum(dist, 0))
    rb = rel_bias.astype(f32)
    far = rb[N_BUCKETS - 1]
    shape = (N_HEADS,) + (1,) * dist.ndim
    out = jnp.zeros((N_HEADS,) + dist.shape, f32)
    for b in range(N_BUCKETS):
        out = jnp.where(bucket[None] == b, (rb[b] - far).reshape(shape), out)
    return out * LOG2E


def _pad_w_in(w_in):
    splits = (512, 512, 512, 512, 512, 512, 256, 64, 4, 1024, 1024)
    offs = np.cumsum((0,) + splits)
    part = {n: w_in[:, offs[i]:offs[i + 1]] for i, n in enumerate(
        ("u", "gp", "q", "k", "v", "ga", "qi", "ki", "wi", "gate_p", "gate_a"))}
    zpad = jnp.zeros((w_in.shape[0], LANES - N_IDX_HEADS), w_in.dtype)
    cols = [part["u"], part["gp"], part["q"], part["k"], part["v"], part["ga"],
            part["qi"] * (IDX_DIM ** -0.5), part["ki"], part["ki"],
            part["wi"] * (N_IDX_HEADS ** -0.5), zpad, part["gate_p"], part["gate_a"]]
    w = jnp.concatenate(cols, axis=1)
    assert w.shape[1] == W_COLS
    return w.astype(bf16)


def kernel(x_prompt, x_sample, cache_k, cache_v, cache_kidx, state_pool, page_table, meta, w_in,
           pool_w, pool_scale, w_br_pool, w_br_attn, rel_bias, w_out, ln_g, ln_b):
    bsz, seq, _ = x_prompt.shape
    dbsz, tnew, _ = x_sample.shape
    n_phys, page = cache_k.shape[0], cache_k.shape[1]
    n_pages = page_table.shape[1]
    past_len = n_pages * page
    assert (IDX_DIM ** -0.5, N_IDX_HEADS ** -0.5) == (0.125, 0.5)
    assert tnew <= SROWS and N_META <= LANES and tnew <= POOL_BUF

    w_pad = _pad_w_in(w_in)
    weights = (pool_w.astype(bf16), pool_scale.reshape(1, POOL_WIDTH).astype(f32),
               w_br_pool.astype(bf16), w_br_attn.astype(bf16), w_out.astype(bf16),
               ln_g.reshape(1, D_MODEL).astype(f32), ln_b.reshape(1, D_MODEL).astype(f32))

    (u, gp, q, k, kb, v, vb, ga, qi, kidx, kib, wi, gtp, gta) = _project(
        x_prompt.reshape(bsz * seq, D_MODEL), w_pad, TQ)
    (u_m, _, _, k_m, kb_m, v_m, vb_m, _, _, kidx_m, kib_m, _, _, _) = _project(
        meta.astype(x_prompt.dtype), w_pad, N_META)
    r3 = lambda a: a.reshape(bsz, seq, a.shape[-1])
    pad_rows = lambda a: jnp.pad(a, ((0, LANES - N_META), (0, 0)))

    ar = jnp.arange(TQ)
    assert TQ >= MAX_DISTANCE
    d_pp = jnp.stack([ar[:, None] - ar[None, :] + TQ * dd for dd in range(2)])
    bias_pp = jnp.swapaxes(_bias_table(rel_bias, d_pp), 0, 1)
    d_m = ar[:, None] + N_META - jnp.arange(LANES)[None, :]
    bias_meta = _bias_table(rel_bias, d_m)

    topk_p = min(TOPK_MAX, seq // 4)
    attn_o = _attn_prompt(r3(qi), r3(wi), r3(q), r3(kib), r3(kb), r3(vb),
                          pad_rows(kib_m), pad_rows(kb_m), pad_rows(vb_m), bias_pp, bias_meta, topk_p)
    y_prompt = _merge_prompt(x_prompt, r3(u), u_m, r3(gp), attn_o, r3(ga), r3(gtp), r3(gta), weights)

    def with_meta(m, a):
        mm = jnp.broadcast_to(m[None], (bsz,) + m.shape)
        return jnp.concatenate([mm, r3(a)], axis=1)

    k_prompt = with_meta(k_m, k).reshape(bsz, seq + N_META, N_HEADS, HEAD_DIM)
    v_prompt = with_meta(v_m, v).reshape(bsz, seq + N_META, N_HEADS, HEAD_DIM)
    kidx_prompt = with_meta(kidx_m, kidx)
    pool_prompt = r3(u)[:, seq - POOL_BUF:]

    (us, gps, qs, ks, kbs, vs, vbs, gas, qis, kidxs, kibs, wis, gtps, gtas) = _project(
        x_sample.reshape(dbsz * tnew, D_MODEL), w_pad, min(TQ, dbsz * tnew))
    s3 = lambda a: a.reshape(dbsz, tnew, a.shape[-1])
    padq = lambda a: jnp.pad(a, ((0, 0), (0, 0), (0, SROWS - tnew), (0, 0)))
    qi4 = jnp.swapaxes(s3(qis).reshape(dbsz, tnew, N_IDX_HEADS, IDX_DIM), 1, 2)
    qi32 = jnp.pad(padq(qi4), ((0, 0), (0, 0), (0, 0), (0, LANES - IDX_DIM)))
    qi32 = qi32.reshape(dbsz, N_IDX_HEADS * SROWS, LANES)
    wi4 = jnp.swapaxes(s3(wis)[:, :, :N_IDX_HEADS], 1, 2)[..., None]
    wb32 = jnp.broadcast_to(padq(wi4), (dbsz, N_IDX_HEADS, SROWS, LANES))
    wb32 = wb32.reshape(dbsz, N_IDX_HEADS * SROWS, LANES)
    kin2 = jnp.pad(s3(kibs), ((0, 0), (0, LANES - tnew), (0, 0)))
    kidx_t = jnp.swapaxes(cache_kidx, 1, 2)
    k_t = jnp.transpose(cache_k, (0, 2, 3, 1)).reshape(n_phys, ATTN_WIDTH, page)
    v_t = jnp.transpose(cache_v, (0, 2, 3, 1)).reshape(n_phys, ATTN_WIDTH, page)
    scores_past, scores_new = _sample_scores(page_table, qi32, wb32, kin2, kidx_t)

    topk_s = min(TOPK_MAX, (past_len + tnew) // 4)
    mask_past, mask_new = _sample_select(scores_past.reshape(dbsz * SROWS, past_len),
                                         scores_new.reshape(dbsz * SROWS, LANES), tnew, topk_s)

    head_of_lane = jnp.arange(ATTN_WIDTH) // HEAD_DIM
    q4 = jnp.where(head_of_lane[None, None, None, :] == jnp.arange(N_HEADS)[None, :, None, None],
                   s3(qs)[:, None, :, :], jnp.zeros((), bf16))
    q64 = padq(q4).reshape(dbsz, N_HEADS * SROWS, ATTN_WIDTH)
    new_t = lambda a: jnp.swapaxes(jnp.pad(s3(a), ((0, 0), (0, LANES - tnew), (0, 0))), 1, 2)
    qrow = jnp.arange(SROWS)
    assert page >= MAX_DISTANCE
    d_s = (page + qrow)[:, None] - jnp.arange(LANES)[None, :]
    bias_s = _bias_table(rel_bias, d_s).reshape(N_HEADS * SROWS, LANES)
    d_n = qrow[:, None] - jnp.arange(LANES)[None, :]
    bias_n = _bias_table(rel_bias, d_n).reshape(N_HEADS * SROWS, LANES)
    ao_s = _sample_attend(page_table, q64,
                          mask_past.reshape(dbsz, SROWS, past_len), mask_new.reshape(dbsz, SROWS, LANES),
                          new_t(kbs), new_t(vbs), bias_s, bias_n, k_t, v_t)
    ao_s = ao_s[:, :tnew].reshape(dbsz, tnew * ATTN_WIDTH)

    flat = lambda a: a.reshape(dbsz, -1)
    y_sample = _merge_sample(flat(x_sample), flat(us), flat(state_pool.astype(f32)), flat(gps), ao_s,
                             flat(gas), flat(gtps), flat(gtas), weights, tnew)
    y_sample = y_sample.reshape(dbsz, tnew, D_MODEL)

    k_sample = ks.reshape(dbsz, tnew, N_HEADS, HEAD_DIM)
    v_sample = vs.reshape(dbsz, tnew, N_HEADS, HEAD_DIM)
    kidx_sample = s3(kidxs)
    pool_sample = jnp.concatenate([state_pool.astype(f32), s3(us)], axis=1)[:, -POOL_BUF:]

    return (y_prompt, y_sample, k_prompt, v_prompt, kidx_prompt, pool_prompt,
            k_sample, v_sample, kidx_sample, pool_sample)
```

```python
import functools
import math

import numpy as np
import jax
import jax.numpy as jnp
from jax import lax
from jax.experimental import pallas as pl
from jax.experimental.pallas import tpu as pltpu

f32 = jnp.float32
bf16 = jnp.bfloat16
i32 = jnp.int32

N_META = 16
POOL_WINDOWS = (2, 4, 8, 16)
POOL_BUF = max(POOL_WINDOWS) - 1
N_HEADS = 8
HEAD_DIM = 64
N_IDX_HEADS = 4
IDX_DIM = 64
TOPK_MAX = 256
N_BUCKETS = 32
MAX_DISTANCE = 128
DEPTH = 1
ALPHA = (2 * DEPTH) ** 0.25
LN_EPS = 1e-5
MASKED_LOGIT = -1e30

LANES = 128
SUBLANES = 8
BF16_ROWS = 16
WORD_BITS = 32
VMEM_LIMIT = 56 * 1024 * 1024

TQ = 256
GROUP_KEYS = WORD_BITS * LANES
INT_MIN = -(2 ** 31)
LOG2E = math.log2(math.e)
Q_SCALE = LOG2E * HEAD_DIM ** -0.5

D_MODEL = 1024
POOL_WIDTH = 512
ATTN_WIDTH = N_HEADS * HEAD_DIM
PAIR = 2 * HEAD_DIM

SEG = {
    "u": (0, 512), "gp": (512, 512), "k": (1024, 512), "ga": (1536, 512), "ki2": (2048, 128),
    "gate_p": (2176, 1024), "gate_a": (3200, 1024),
}
W_COLS = 4224
TSEG = {"kT": (0, 512), "vT": (512, 512), "qT": (1024, 512), "qiT": (1536, 256),
        "kidxT": (1792, IDX_DIM), "wiT": (1792 + IDX_DIM, BF16_ROWS)}
WT_ROWS = 1792 + IDX_DIM + BF16_ROWS
NT_DIMS = (((1,), (1,)), ((), ()))


def _cparams(n_axes):
    return pltpu.CompilerParams(dimension_semantics=("arbitrary",) * n_axes,
                                vmem_limit_bytes=VMEM_LIMIT)


def _project_body(x_ref, w_ref, wt_ref, u_o, gp_o, kb_o, ga_o, kib_o, gtp_o, gta_o,
                  kt_o, vt_o, vtb_o, qt_o, qit_o, kidxt_o, wit_o):
    xb = x_ref[...].astype(bf16)

    def seg(name):
        off, n = SEG[name]
        return jnp.dot(xb, w_ref[:, off:off + n], preferred_element_type=f32)

    def tseg(name):
        off, n = TSEG[name]
        return lax.dot_general(wt_ref[off:off + n, :], xb, NT_DIMS, preferred_element_type=f32)

    u_o[...] = seg("u")
    gp_o[...] = seg("gp")
    kb_o[...] = seg("k").astype(bf16)
    ga_o[...] = seg("ga")
    kib_o[...] = seg("ki2").astype(bf16)
    gtp_o[...] = seg("gate_p")
    gta_o[...] = seg("gate_a")
    kt_o[...] = tseg("kT")
    vt = tseg("vT")
    vt_o[...] = vt
    vtb_o[...] = vt.astype(bf16)
    qt_o[...] = (tseg("qT") * Q_SCALE).astype(bf16)
    qit_o[...] = tseg("qiT").astype(bf16)
    kidxt_o[...] = tseg("kidxT")
    wit_o[...] = tseg("wiT")[0:SUBLANES]


def _project(x2d, w_pad, wt_pad, tm):
    rows = x2d.shape[0]
    assert rows % tm == 0
    row_outs = [(512, f32), (512, f32), (512, bf16), (512, f32), (128, bf16), (1024, f32), (1024, f32)]
    col_outs = [(512, f32), (512, f32), (512, bf16), (512, bf16), (256, bf16), (IDX_DIM, f32),
                (SUBLANES, f32)]
    out_shape = ([jax.ShapeDtypeStruct((rows, n), dt) for n, dt in row_outs]
                 + [jax.ShapeDtypeStruct((n, rows), dt) for n, dt in col_outs])
    out_specs = ([pl.BlockSpec((tm, n), lambda i: (i, 0)) for n, _ in row_outs]
                 + [pl.BlockSpec((n, tm), lambda i: (0, i)) for n, _ in col_outs])
    return pl.pallas_call(
        _project_body,
        grid=(rows // tm,),
        in_specs=[pl.BlockSpec((tm, D_MODEL), lambda i: (i, 0)),
                  pl.BlockSpec((D_MODEL, W_COLS), lambda i: (0, 0)),
                  pl.BlockSpec((WT_ROWS, D_MODEL), lambda i: (0, 0))],
        out_specs=out_specs,
        out_shape=out_shape,
        compiler_params=_cparams(1),
    )(x2d, w_pad, wt_pad)


def _score_key(score):
    bits = lax.bitcast_convert_type(score, i32)
    sign = lax.shift_right_arithmetic(bits, jnp.full(bits.shape, 31, i32))
    mag = bits & jnp.int32(0x7FFFFFFF)
    return (mag ^ sign) - sign


def _bit_transpose32(words):
    a = list(words)
    j = 16
    m = 0x0000FFFF
    while j != 0:
        k = 0
        sh = jnp.full(a[0].shape, j, i32)
        mm = jnp.int32(np.uint32(m).astype(np.int32))
        while k < 32:
            t = (a[k] ^ lax.shift_right_logical(a[k + j], sh)) & mm
            a[k] = a[k] ^ t
            a[k + j] = a[k + j] ^ lax.shift_left(t, sh)
            k = (k + j + 1) & ~j
        j >>= 1
        if j:
            m = (m ^ (m << j)) & 0xFFFFFFFF
    return a


def _bit_planes(words):
    t = _bit_transpose32(words[::-1])
    planes = t[::-1]
    planes[WORD_BITS - 1] = ~planes[WORD_BITS - 1]
    return planes


def _transpose_group(planes_scr, g, rows_total):
    def body(rg, carry):
        rows = pl.ds(pl.multiple_of(rg * SUBLANES, SUBLANES), SUBLANES)
        planes = _bit_planes([planes_scr[g, c, rows, :] for c in range(WORD_BITS)])
        for b in range(WORD_BITS):
            planes_scr[g, b, rows, :] = planes[b]
        return carry

    lax.fori_loop(0, rows_total // SUBLANES, body, 0)


def _lane_total(cnt, ones_mat):
    return jnp.dot(cnt.astype(f32).astype(bf16), ones_mat, preferred_element_type=f32)


def _radix_select(planes_scr, single_key, alive_scr, sel_scr, krem_scr, n_groups, single_first,
                  alive_init, k_row):
    rows = single_key.shape[0]
    nset = n_groups + 1
    ones_mat = jnp.ones((LANES, LANES), bf16)
    zero = jnp.zeros((rows, LANES), i32)

    for s in range(nset):
        alive_scr[s] = alive_init[s]
        sel_scr[s] = zero
    krem_scr[...] = k_row

    def step(planes):
        alive = [alive_scr[s] for s in range(nset)]
        ones = [alive[s] & planes[s] for s in range(nset)]
        cnt = ones[0]
        for s in range(1, nset):
            cnt = cnt + lax.population_count(ones[s])
        tot = _lane_total(cnt, ones_mat)
        krem = krem_scr[...]
        take = tot >= krem
        for s in range(nset):
            alive_scr[s] = jnp.where(take, ones[s], alive[s] ^ ones[s])
            sel_scr[s] = sel_scr[s] | jnp.where(take, zero, ones[s])
        krem_scr[...] = jnp.where(take, krem, krem - tot)

    def value_step(t, carry):
        b = WORD_BITS - 1 - t
        sh = jnp.full((rows, LANES), b, i32)
        planes = [lax.shift_right_logical(single_key, sh) & 1]
        for g in range(n_groups):
            planes.append(planes_scr[g, b])
        step(planes)
        return carry

    lax.fori_loop(0, WORD_BITS, value_step, 0)

    full = jnp.full((rows, LANES), -1, i32)
    one = jnp.full((rows, LANES), 1, i32)
    codes = [0] + [g + 1 for g in range(n_groups)] if single_first else [n_groups] + list(range(n_groups))
    n_code_bits = max(1, (max(codes)).bit_length())
    for cb in reversed(range(n_code_bits)):
        planes = []
        for s in range(nset):
            pref = ((codes[s] >> cb) & 1) == 0
            planes.append((one if s == 0 else full) if pref else zero)
        step(planes)
    for cm in CHUNK_MASKS:
        word = jnp.full((rows, LANES), np.uint32(cm).astype(np.int32), i32)
        step([one] + [word] * n_groups)
    lane = lax.broadcasted_iota(i32, (rows, LANES), 1)
    for lb in reversed(range(7)):
        low = ((lane >> lb) & 1) == 0
        lw = jnp.where(low, full, zero)
        step([lw & 1] + [lw] * n_groups)
    for s in range(nset):
        sel_scr[s] = sel_scr[s] | alive_scr[s]


CHUNK_MASKS = (0x0000FFFF, 0x00FF00FF, 0x0F0F0F0F, 0x33333333, 0x55555555)


def _mask_from_bits(word, bit):
    sh = jnp.full(word.shape, bit, i32)
    picked = (lax.shift_right_logical(word, sh) & 1) != 0
    return jnp.where(picked, 0.0, MASKED_LOGIT).astype(f32)


N_KG = TQ // SUBLANES
META_KG = N_META // SUBLANES
META_TILE = LANES
ONES_ROWS = BF16_ROWS
TILE_BUCKETS = (8, 16, 24, 32)


class _RefList:
    def __init__(self, refs):
        self.refs = list(refs)

    def _split(self, idx):
        idx = idx if isinstance(idx, tuple) else (idx,)
        return self.refs[idx[0]], (idx[1:] if len(idx) > 1 else (Ellipsis,))

    def __getitem__(self, idx):
        ref, rest = self._split(idx)
        return ref[rest]

    def __setitem__(self, idx, val):
        ref, rest = self._split(idx)
        ref[rest] = val


def _all_sublanes(x, op):
    return jnp.broadcast_to(op(x, axis=0, keepdims=True), x.shape)


def _select_prompt(planes_scr, keym_scr, alive_scr, sel_scr, alivem_scr, selm_scr, krem_scr,
                   n_tiles, i, k_row):
    tile_id = lax.broadcasted_iota(i32, (n_tiles, SUBLANES, TQ), 0)
    zero_t = jnp.zeros((n_tiles, SUBLANES, TQ), i32)
    full_t = jnp.full((n_tiles, SUBLANES, TQ), -1, i32)
    zero_m = jnp.zeros((META_KG, SUBLANES, TQ), i32)
    one_m = jnp.full((META_KG, SUBLANES, TQ), 1, i32)
    alive_scr[0:n_tiles] = jnp.where(tile_id <= i, full_t, zero_t)
    sel_scr[0:n_tiles] = zero_t
    alivem_scr[...] = one_m
    selm_scr[...] = zero_m
    krem_scr[...] = k_row

    def step(pm, pt):
        am = alivem_scr[...]
        at = alive_scr[0:n_tiles]
        om = am & pm
        ot = at & pt
        cnt = jnp.sum(lax.population_count(ot), axis=0) + jnp.sum(om, axis=0)
        tot = _all_sublanes(cnt.astype(f32), jnp.sum)
        krem = krem_scr[...]
        take = tot >= krem
        alive_scr[0:n_tiles] = jnp.where(take[None], ot, at ^ ot)
        sel_scr[0:n_tiles] = sel_scr[0:n_tiles] | jnp.where(take[None], zero_t, ot)
        alivem_scr[...] = jnp.where(take[None], om, am ^ om)
        selm_scr[...] = selm_scr[...] | jnp.where(take[None], zero_m, om)
        krem_scr[...] = jnp.where(take, krem, krem - tot)

    def value_step(t, carry):
        b = WORD_BITS - 1 - t
        sh = jnp.full((META_KG, SUBLANES, TQ), b, i32)
        step(lax.shift_right_logical(keym_scr[...], sh) & 1, planes_scr[0:n_tiles, b])
        return carry

    lax.fori_loop(0, WORD_BITS, value_step, 0)

    step(one_m, zero_t)
    for tb in reversed(range((n_tiles - 1).bit_length())):
        step(one_m, jnp.where(((tile_id >> tb) & 1) == 0, full_t, zero_t))
    for cm in CHUNK_MASKS:
        step(one_m, jnp.full((n_tiles, SUBLANES, TQ), np.uint32(cm).astype(np.int32), i32))
    kg_m = lax.broadcasted_iota(i32, (META_KG, SUBLANES, TQ), 0)
    for gb in reversed(range((META_KG - 1).bit_length())):
        step(jnp.where(((kg_m >> gb) & 1) == 0, one_m, zero_m), full_t)
    sub_t = lax.broadcasted_iota(i32, (n_tiles, SUBLANES, TQ), 1)
    sub_m = lax.broadcasted_iota(i32, (META_KG, SUBLANES, TQ), 1)
    for sb in reversed(range(3)):
        step(jnp.where(((sub_m >> sb) & 1) == 0, one_m, zero_m),
             jnp.where(((sub_t >> sb) & 1) == 0, full_t, zero_t))
    sel_scr[0:n_tiles] = sel_scr[0:n_tiles] | alive_scr[0:n_tiles]
    selm_scr[...] = selm_scr[...] | alivem_scr[...]


def _attn_prompt_body(qit_ref, wit_ref, qt_ref, ki_ref, k_ref, vt_ref, kim_ref, km_ref, vmt_ref,
                      bpp_ref, bmeta_ref, o_ref,
                      planes_scr, keym_scr, alive_scr, sel_scr, alivem_scr, selm_scr, krem_scr,
                      keyw_scr, qm_scr, madd_scr, ot_scr, *per_head, n_tiles, topk):
    m_scr, l_scr, acc_scr, s_scr, p_scr, alpha_scr = (
        _RefList(per_head[N_HEADS * g:N_HEADS * (g + 1)]) for g in range(6))
    i = pl.program_id(1)
    halves = [slice(LANES * qh, LANES * (qh + 1)) for qh in range(TQ // LANES)]
    zhalf = jnp.zeros((HEAD_DIM, TQ), bf16)

    qim = [jnp.concatenate([qit_ref[IDX_DIM * h:IDX_DIM * (h + 1), :], zhalf], axis=0)
           for h in range(N_IDX_HEADS)]
    wrow = [wit_ref[h:h + 1, :] for h in range(N_IDX_HEADS)]

    def score_block(kt2):
        acc = None
        for h in range(N_IDX_HEADS):
            s = jnp.dot(kt2, qim[h], preferred_element_type=f32)
            t = jnp.maximum(s, 0.0) * wrow[h]
            acc = t if acc is None else acc + t
        return acc

    def store_planes(t):
        for cols in halves:
            planes = _bit_planes([keyw_scr[SUBLANES * kg:SUBLANES * (kg + 1), cols]
                                  for kg in range(N_KG)])
            for b in range(WORD_BITS):
                planes_scr[t, b, :, cols] = planes[b]

    def key_rows(t):
        return pl.ds(pl.multiple_of(t * TQ, TQ), TQ)

    def score_tile(t, carry):
        keyw_scr[...] = _score_key(score_block(ki_ref[0, key_rows(t), :]))
        store_planes(t)
        return carry

    lax.fori_loop(0, i, score_tile, 0)
    krow = lax.broadcasted_iota(i32, (TQ, TQ), 0)
    qcol = lax.broadcasted_iota(i32, (TQ, TQ), 1)
    keyw_scr[...] = jnp.where(krow <= qcol, _score_key(score_block(ki_ref[0, key_rows(i), :])), INT_MIN)
    store_planes(i)
    keym_scr[...] = (_score_key(score_block(kim_ref[0:N_META, :])) ^ jnp.int32(INT_MIN)
                     ).reshape(META_KG, SUBLANES, TQ)

    qpos = lax.broadcasted_iota(i32, (SUBLANES, TQ), 1)
    k_row = jnp.minimum(i * TQ + qpos + (N_META + 1), topk).astype(f32)
    lo = 0
    for nb in TILE_BUCKETS:
        nb = min(nb, n_tiles)
        if nb <= lo:
            continue

        @pl.when((i >= lo) & (i < nb))
        def _(nb=nb):
            _select_prompt(planes_scr, keym_scr, alive_scr, sel_scr, alivem_scr, selm_scr, krem_scr,
                           nb, i, k_row)
        lo = nb
    assert lo == n_tiles

    for h in range(N_HEADS):
        qh = qt_ref[HEAD_DIM * h:HEAD_DIM * (h + 1), :]
        qm_scr[h] = jnp.concatenate([qh, zhalf] if h % 2 == 0 else [zhalf, qh], axis=0)
    for h in range(N_HEADS):
        m_scr[h] = jnp.full((SUBLANES, TQ), -jnp.inf, f32)
        l_scr[h] = jnp.zeros((SUBLANES, TQ), f32)
        acc_scr[h] = jnp.zeros((HEAD_DIM, TQ), f32)

    def attend(width, k_pair, vt_head, bias_fn):
        n_kg = width // SUBLANES
        ones_rows = jnp.ones((ONES_ROWS, width), bf16)
        for h in range(N_HEADS):
            s_scr[h, 0:width] = jnp.dot(k_pair(h // 2), qm_scr[h], preferred_element_type=f32)
        for h in range(N_HEADS):
            buf = h
            for cols in halves:
                vs = []
                for kg in range(n_kg):
                    rows = slice(SUBLANES * kg, SUBLANES * (kg + 1))
                    v = s_scr[buf, rows, cols] + madd_scr[rows, cols]
                    if bias_fn is not None:
                        v = v + bias_fn(h, rows, cols)
                    vs.append(v)
                mx = vs[0]
                for v in vs[1:]:
                    mx = jnp.maximum(mx, v)
                m_old = m_scr[h, :, cols]
                m_new = jnp.maximum(m_old, _all_sublanes(mx, jnp.max))
                m_scr[h, :, cols] = m_new
                alpha_scr[buf, :, cols] = jnp.exp2(m_old - m_new)
                for kk in range(n_kg // 2):
                    pp = jnp.concatenate([jnp.exp2(vs[2 * kk] - m_new),
                                          jnp.exp2(vs[2 * kk + 1] - m_new)], axis=0)
                    p_scr[buf, BF16_ROWS * kk:BF16_ROWS * (kk + 1), cols] = pp.astype(bf16)
        for h in range(N_HEADS):
            buf = h
            vaug = jnp.concatenate([vt_head(h), ones_rows], axis=0)
            ol = jnp.dot(vaug, p_scr[buf, 0:width], preferred_element_type=f32)
            a = alpha_scr[buf]
            for rg in range(HEAD_DIM // SUBLANES):
                rows = slice(SUBLANES * rg, SUBLANES * (rg + 1))
                acc_scr[h, rows] = a * acc_scr[h, rows] + ol[rows]
            l_scr[h] = a * l_scr[h] + ol[HEAD_DIM:HEAD_DIM + SUBLANES]

    def head_rows(h):
        return slice(HEAD_DIM * h, HEAD_DIM * (h + 1))

    def pair_cols(hp):
        return slice(PAIR * hp, PAIR * (hp + 1))

    for kg in range(META_TILE // SUBLANES):
        rows = slice(SUBLANES * kg, SUBLANES * (kg + 1))
        if kg < META_KG:
            madd_scr[rows] = _mask_from_bits(selm_scr[kg], 0)
        else:
            madd_scr[rows] = jnp.full((SUBLANES, TQ), MASKED_LOGIT, f32)
    meta_k = lambda hp: km_ref[:, pair_cols(hp)]
    meta_v = lambda h: vmt_ref[head_rows(h), :]

    @pl.when(i == 0)
    def _():
        attend(META_TILE, meta_k, meta_v, lambda h, rows, cols: bmeta_ref[h, rows, cols])

    @pl.when(i > 0)
    def _():
        attend(META_TILE, meta_k, meta_v, None)

    def set_mask(t):
        word = sel_scr[t]
        for kg in range(N_KG):
            madd_scr[SUBLANES * kg:SUBLANES * (kg + 1)] = _mask_from_bits(word, kg)

    def tile_k(t):
        return lambda hp: k_ref[0, key_rows(t), pair_cols(hp)]

    def tile_v(t):
        return lambda h: vt_ref[head_rows(h), key_rows(t)]

    def far_tile(t, carry):
        set_mask(t)
        attend(TQ, tile_k(t), tile_v(t), None)
        return carry

    lax.fori_loop(0, i - 1, far_tile, 0)

    @pl.when(i > 0)
    def _():
        set_mask(i - 1)
        attend(TQ, tile_k(i - 1), tile_v(i - 1), lambda h, rows, cols: bpp_ref[1, h, rows, cols])

    set_mask(i)
    attend(TQ, tile_k(i), tile_v(i), lambda h, rows, cols: bpp_ref[0, h, rows, cols])

    for h in range(N_HEADS):
        inv = 1.0 / l_scr[h]
        for rg in range(HEAD_DIM // SUBLANES):
            rows = slice(SUBLANES * rg, SUBLANES * (rg + 1))
            ot_scr[HEAD_DIM * h + SUBLANES * rg:HEAD_DIM * h + SUBLANES * (rg + 1)] = acc_scr[h, rows] * inv
    o_ref[0] = ot_scr[...].T


def _attn_prompt(qit, wit, qt, kib, kb, vtb, kim, km, vmt, bias_pp, bias_meta, topk):
    bsz, seq, _ = kb.shape
    assert seq % TQ == 0
    n_tiles = seq // TQ
    assert n_tiles <= TILE_BUCKETS[-1]
    const = lambda nd: (lambda b, i: (0,) * nd)
    body = functools.partial(_attn_prompt_body, n_tiles=n_tiles, topk=topk)
    one = pl.Buffered(1)
    per_head = lambda shape, dt: [pltpu.VMEM(shape, dt) for _ in range(N_HEADS)]
    return pl.pallas_call(
        body,
        grid=(bsz, n_tiles),
        in_specs=[
            pl.BlockSpec((N_IDX_HEADS * IDX_DIM, TQ), lambda b, i: (0, b * n_tiles + i)),
            pl.BlockSpec((SUBLANES, TQ), lambda b, i: (0, b * n_tiles + i)),
            pl.BlockSpec((ATTN_WIDTH, TQ), lambda b, i: (0, b * n_tiles + i)),
            pl.BlockSpec((1, seq, LANES), lambda b, i: (b, 0, 0), pipeline_mode=one),
            pl.BlockSpec((1, seq, ATTN_WIDTH), lambda b, i: (b, 0, 0), pipeline_mode=one),
            pl.BlockSpec((ATTN_WIDTH, seq), lambda b, i: (0, b), pipeline_mode=one),
            pl.BlockSpec((META_TILE, LANES), const(2)),
            pl.BlockSpec((META_TILE, ATTN_WIDTH), const(2)),
            pl.BlockSpec((ATTN_WIDTH, META_TILE), const(2)),
            pl.BlockSpec((2, N_HEADS, TQ, TQ), const(4), pipeline_mode=one),
            pl.BlockSpec((N_HEADS, META_TILE, TQ), const(3), pipeline_mode=one),
        ],
        out_specs=pl.BlockSpec((1, TQ, ATTN_WIDTH), lambda b, i: (b, i, 0)),
        out_shape=jax.ShapeDtypeStruct((bsz, seq, ATTN_WIDTH), f32),
        scratch_shapes=[
            pltpu.VMEM((n_tiles, WORD_BITS, SUBLANES, TQ), i32),
            pltpu.VMEM((META_KG, SUBLANES, TQ), i32),
            pltpu.VMEM((n_tiles, SUBLANES, TQ), i32),
            pltpu.VMEM((n_tiles, SUBLANES, TQ), i32),
            pltpu.VMEM((META_KG, SUBLANES, TQ), i32),
            pltpu.VMEM((META_KG, SUBLANES, TQ), i32),
            pltpu.VMEM((SUBLANES, TQ), f32),
            pltpu.VMEM((TQ, TQ), i32),
            pltpu.VMEM((N_HEADS, PAIR, TQ), bf16),
            pltpu.VMEM((TQ, TQ), f32),
            pltpu.VMEM((ATTN_WIDTH, TQ), f32),
        ] + per_head((SUBLANES, TQ), f32)
          + per_head((SUBLANES, TQ), f32)
          + per_head((HEAD_DIM, TQ), f32)
          + per_head((TQ, TQ), f32)
          + per_head((TQ, TQ), bf16)
          + per_head((SUBLANES, TQ), f32),
        compiler_params=_cparams(2),
    )(qit, wit, qt, kib, kb, vtb, kim, km, vmt, bias_pp, bias_meta)


PAGES_PER_STEP = 8
IDX_PAGES_PER_STEP = 16
SROWS = 8
SAMPLE_SUB = 16


def _sample_scores_body(pt_ref, qi_ref, wb_ref, kin_ref, *rest):
    page_refs = rest[:IDX_PAGES_PER_STEP]
    out_ref, outn_ref = rest[IDX_PAGES_PER_STEP:]
    qi = qi_ref[0]
    wb = wb_ref[0]

    def combine(s):
        t = jnp.maximum(s, 0.0) * wb
        acc = t[0:SROWS]
        for h in range(1, N_IDX_HEADS):
            acc = acc + t[SROWS * h:SROWS * (h + 1)]
        return acc

    zpad = jnp.zeros((LANES - IDX_DIM, LANES), bf16)
    for jx in range(IDX_PAGES_PER_STEP):
        kt = jnp.concatenate([page_refs[jx][0].astype(bf16), zpad], axis=0)
        out_ref[0, :, LANES * jx:LANES * (jx + 1)] = combine(
            jnp.dot(qi, kt, preferred_element_type=f32))
    outn_ref[0] = combine(lax.dot_general(qi, kin_ref[0], NT_DIMS, preferred_element_type=f32))


def _sample_scores(page_table, qi32, wb32, kin2, kidx_t):
    dbsz, n_pages = page_table.shape
    page = kidx_t.shape[2]
    assert page == LANES and n_pages % IDX_PAGES_PER_STEP == 0
    nsteps = n_pages // IDX_PAGES_PER_STEP
    mrows = N_IDX_HEADS * SROWS

    def page_spec(jx):
        return pl.BlockSpec((1, IDX_DIM, page),
                            lambda b, s, pt: (pt[b, s * IDX_PAGES_PER_STEP + jx], 0, 0))

    grid_spec = pltpu.PrefetchScalarGridSpec(
        num_scalar_prefetch=1,
        grid=(dbsz, nsteps),
        in_specs=[pl.BlockSpec((1, mrows, LANES), lambda b, s, pt: (b, 0, 0)),
                  pl.BlockSpec((1, mrows, LANES), lambda b, s, pt: (b, 0, 0)),
                  pl.BlockSpec((1, LANES, LANES), lambda b, s, pt: (b, 0, 0))]
                 + [page_spec(jx) for jx in range(IDX_PAGES_PER_STEP)],
        out_specs=[pl.BlockSpec((1, SROWS, IDX_PAGES_PER_STEP * LANES), lambda b, s, pt: (b, 0, s)),
                   pl.BlockSpec((1, SROWS, LANES), lambda b, s, pt: (b, 0, 0))],
    )
    return pl.pallas_call(
        _sample_scores_body,
        grid_spec=grid_spec,
        out_shape=[jax.ShapeDtypeStruct((dbsz, SROWS, n_pages * page), f32),
                   jax.ShapeDtypeStruct((dbsz, SROWS, LANES), f32)],
        compiler_params=_cparams(2),
    )(page_table, qi32, wb32, kin2, *([kidx_t] * IDX_PAGES_PER_STEP))


SEL_ROWS = 128


def _sample_select_body(sp_ref, sn_ref, mp_ref, mn_ref, planes_scr, alive_scr, sel_scr, krem_scr,
                        *, n_groups, n_new, topk):
    n_chunks = sp_ref.shape[1] // LANES

    def key_chunk(c, carry):
        col = pl.ds(pl.multiple_of(c * LANES, LANES), LANES)
        planes_scr[c // WORD_BITS, c % WORD_BITS] = _score_key(sp_ref[:, col])
        return carry

    lax.fori_loop(0, n_chunks, key_chunk, 0)
    for c in range(n_chunks, n_groups * WORD_BITS):
        planes_scr[c // WORD_BITS, c % WORD_BITS] = jnp.full((SEL_ROWS, LANES), INT_MIN, i32)
    lane = lax.broadcasted_iota(i32, (SEL_ROWS, LANES), 1)
    row = lax.broadcasted_iota(i32, (SEL_ROWS, LANES), 0)
    causal_new = (lane <= (row & (SROWS - 1))) & (lane < n_new)
    key_n = jnp.where(causal_new, _score_key(sn_ref[...]), INT_MIN) ^ jnp.int32(INT_MIN)

    for g in range(n_groups):
        _transpose_group(planes_scr, g, SEL_ROWS)
    full = jnp.full((SEL_ROWS, LANES), -1, i32)
    alive_init = [jnp.full((SEL_ROWS, LANES), 1, i32)] + [full] * n_groups
    k_row = jnp.full((SEL_ROWS, LANES), topk, f32)
    _radix_select(planes_scr, key_n, alive_scr, sel_scr, krem_scr, n_groups, False, alive_init, k_row)

    def mask_chunk(c, carry):
        col = pl.ds(pl.multiple_of(c * LANES, LANES), LANES)
        mp_ref[:, col] = _mask_from_bits(sel_scr[1 + c // WORD_BITS], c % WORD_BITS)
        return carry

    lax.fori_loop(0, n_chunks, mask_chunk, 0)
    mn_ref[...] = _mask_from_bits(sel_scr[0], 0)


def _sample_select(scores_past, scores_new, n_new, topk):
    rows, past = scores_past.shape
    assert rows % SEL_ROWS == 0 and past % LANES == 0
    n_groups = -(-past // GROUP_KEYS)
    body = functools.partial(_sample_select_body, n_groups=n_groups, n_new=n_new, topk=topk)
    return pl.pallas_call(
        body,
        grid=(rows // SEL_ROWS,),
        in_specs=[pl.BlockSpec((SEL_ROWS, past), lambda r: (r, 0)),
                  pl.BlockSpec((SEL_ROWS, LANES), lambda r: (r, 0))],
        out_specs=[pl.BlockSpec((SEL_ROWS, past), lambda r: (r, 0)),
                   pl.BlockSpec((SEL_ROWS, LANES), lambda r: (r, 0))],
        out_shape=[jax.ShapeDtypeStruct((rows, past), f32),
                   jax.ShapeDtypeStruct((rows, LANES), f32)],
        scratch_shapes=[
            pltpu.VMEM((n_groups, WORD_BITS, SEL_ROWS, LANES), i32),
            pltpu.VMEM((n_groups + 1, SEL_ROWS, LANES), i32),
            pltpu.VMEM((n_groups + 1, SEL_ROWS, LANES), i32),
            pltpu.VMEM((SEL_ROWS, LANES), f32),
        ],
        compiler_params=_cparams(1),
    )(scores_past, scores_new)


def _sample_attend_body(pt_ref, q_ref, mp_ref, mn_ref, knt_ref, vnt_ref, bs_ref, bn_ref, *rest):
    k_refs = rest[:PAGES_PER_STEP]
    v_refs = rest[PAGES_PER_STEP:2 * PAGES_PER_STEP]
    o_ref, m_scr, l_scr, acc_scr, kcat_scr, vcat_scr, s_scr, p_scr, alpha_scr = rest[2 * PAGES_PER_STEP:]
    s_idx = pl.program_id(1)
    last_step = s_idx == pl.num_programs(1) - 1
    mrows = N_HEADS * SROWS
    q = q_ref[0]

    @pl.when(s_idx == 0)
    def _():
        m_scr[...] = jnp.full((mrows, LANES), -jnp.inf, f32)
        l_scr[...] = jnp.zeros((mrows, LANES), f32)
        acc_scr[...] = jnp.zeros((mrows, ATTN_WIDTH), f32)

    def attend(width, bias_tail, maskadd):
        s_scr[:, 0:width] = jnp.dot(q, kcat_scr[:, 0:width], preferred_element_type=f32)
        mk = jnp.concatenate([maskadd] * (SAMPLE_SUB // SROWS), axis=0)
        for r0 in range(0, mrows, SAMPLE_SUB):
            rows = slice(r0, r0 + SAMPLE_SUB)
            s = s_scr[rows, 0:width] + mk
            tail = s[:, width - LANES:] + bias_tail[rows]
            s = tail if width == LANES else jnp.concatenate([s[:, :width - LANES], tail], axis=1)
            m_old = m_scr[rows]
            m_new = jnp.maximum(m_old, jnp.max(s, axis=-1, keepdims=True))
            alpha = jnp.exp2(m_old - m_new)
            m_scr[rows] = m_new
            m_rep = m_new if width == LANES else jnp.concatenate([m_new] * (width // LANES), axis=1)
            p = jnp.exp2(s - m_rep)
            l_scr[rows] = alpha * l_scr[rows] + jnp.sum(p, axis=-1, keepdims=True)
            p_scr[rows, 0:width] = p.astype(bf16)
            alpha_scr[rows] = alpha
        o = lax.dot_general(p_scr[:, 0:width], vcat_scr[:, 0:width], NT_DIMS, preferred_element_type=f32)
        aw = jnp.concatenate([alpha_scr[...]] * (ATTN_WIDTH // LANES), axis=1)
        acc_scr[...] = aw * acc_scr[...] + o

    for jx in range(PAGES_PER_STEP):
        kcat_scr[:, LANES * jx:LANES * (jx + 1)] = k_refs[jx][0].astype(bf16)
        vcat_scr[:, LANES * jx:LANES * (jx + 1)] = v_refs[jx][0].astype(bf16)
    attend(PAGES_PER_STEP * LANES, bs_ref[...] * jnp.where(last_step, 1.0, 0.0), mp_ref[0])

    @pl.when(last_step)
    def _():
        kcat_scr[:, 0:LANES] = knt_ref[0]
        vcat_scr[:, 0:LANES] = vnt_ref[0]
        attend(LANES, bn_ref[...], mn_ref[0])
        l_w = jnp.concatenate([l_scr[...]] * (ATTN_WIDTH // LANES), axis=1)
        accn = acc_scr[...] / l_w
        head_of_lane = lax.broadcasted_iota(i32, (SROWS, ATTN_WIDTH), 1) // HEAD_DIM
        out = jnp.zeros((SROWS, ATTN_WIDTH), f32)
        for h in range(N_HEADS):
            out = out + jnp.where(head_of_lane == h, accn[SROWS * h:SROWS * (h + 1)], 0.0)
        o_ref[0] = out


def _sample_attend(page_table, q64, mask_past, mask_new, knt, vnt, bias_s, bias_n, k_t, v_t):
    dbsz, n_pages = page_table.shape
    page = k_t.shape[2]
    assert page == LANES and n_pages % PAGES_PER_STEP == 0
    nsteps = n_pages // PAGES_PER_STEP
    mrows = N_HEADS * SROWS
    width = PAGES_PER_STEP * LANES

    def page_spec(jx):
        return pl.BlockSpec((1, ATTN_WIDTH, page), lambda b, s, pt: (pt[b, s * PAGES_PER_STEP + jx], 0, 0))

    grid_spec = pltpu.PrefetchScalarGridSpec(
        num_scalar_prefetch=1,
        grid=(dbsz, nsteps),
        in_specs=[pl.BlockSpec((1, mrows, ATTN_WIDTH), lambda b, s, pt: (b, 0, 0)),
                  pl.BlockSpec((1, SROWS, width), lambda b, s, pt: (b, 0, s)),
                  pl.BlockSpec((1, SROWS, LANES), lambda b, s, pt: (b, 0, 0)),
                  pl.BlockSpec((1, ATTN_WIDTH, LANES), lambda b, s, pt: (b, 0, 0)),
                  pl.BlockSpec((1, ATTN_WIDTH, LANES), lambda b, s, pt: (b, 0, 0)),
                  pl.BlockSpec((mrows, LANES), lambda b, s, pt: (0, 0)),
                  pl.BlockSpec((mrows, LANES), lambda b, s, pt: (0, 0))]
                 + [page_spec(jx) for jx in range(PAGES_PER_STEP)]
                 + [page_spec(jx) for jx in range(PAGES_PER_STEP)],
        out_specs=pl.BlockSpec((1, SROWS, ATTN_WIDTH), lambda b, s, pt: (b, 0, 0)),
        scratch_shapes=[pltpu.VMEM((mrows, LANES), f32),
                        pltpu.VMEM((mrows, LANES), f32),
                        pltpu.VMEM((mrows, ATTN_WIDTH), f32),
                        pltpu.VMEM((ATTN_WIDTH, width), bf16),
                        pltpu.VMEM((ATTN_WIDTH, width), bf16),
                        pltpu.VMEM((mrows, width), f32),
                        pltpu.VMEM((mrows, width), bf16),
                        pltpu.VMEM((mrows, LANES), f32)],
    )
    return pl.pallas_call(
        _sample_attend_body,
        grid_spec=grid_spec,
        out_shape=jax.ShapeDtypeStruct((dbsz, SROWS, ATTN_WIDTH), f32),
        compiler_params=_cparams(2),
    )(page_table, q64, mask_past, mask_new, knt, vnt, bias_s, bias_n,
      *([k_t] * PAGES_PER_STEP), *([v_t] * PAGES_PER_STEP))


def _silu(x):
    return x * jax.nn.sigmoid(x)


def _merge_core(x, d, gp, ao, ga, gtp, gta, pw_ref, ps_ref, wbp_ref, wba_ref, wo_ref, lng_ref, lnb_ref):
    gw = POOL_WIDTH // len(POOL_WINDOWS)
    parts = [jnp.dot(d[:, gw * g:gw * (g + 1)].astype(bf16), pw_ref[g], preferred_element_type=f32)
             for g in range(len(POOL_WINDOWS))]
    pool_o = jnp.concatenate(parts, axis=1) * ps_ref[...]
    bp = jnp.dot((pool_o * _silu(gp)).astype(bf16), wbp_ref[...], preferred_element_type=f32)
    ba = jnp.dot((ao * _silu(ga)).astype(bf16), wba_ref[...], preferred_element_type=f32)
    m = jax.nn.sigmoid(gtp) * bp + jax.nn.sigmoid(gta) * ba
    out = jnp.dot(m.astype(bf16), wo_ref[...], preferred_element_type=f32)
    z = ALPHA * x + out
    mu = jnp.mean(z, axis=-1, keepdims=True)
    zc = z - mu
    var = jnp.mean(zc * zc, axis=-1, keepdims=True)
    return zc * lax.rsqrt(var + LN_EPS) * lng_ref[...] + lnb_ref[...]


def _merge_prompt_body(x_ref, u_ref, uh_ref, um_ref, gp_ref, ao_ref, ga_ref, gtp_ref, gta_ref,
                       pw_ref, ps_ref, wbp_ref, wba_ref, wo_ref, lng_ref, lnb_ref, y_ref, uu_scr):
    t = pl.program_id(1)
    halo = N_META
    uu_scr[0:halo, :] = jnp.where(t == 0, um_ref[...], uh_ref[0])
    uu_scr[halo:halo + TQ, :] = u_ref[0]
    gw = POOL_WIDTH // len(POOL_WINDOWS)
    parts = []
    for g, w in enumerate(POOL_WINDOWS):
        cols = slice(gw * g, gw * (g + 1))
        s = uu_scr[halo:halo + TQ, cols]
        for j in range(1, w):
            s = s + uu_scr[halo - j:halo - j + TQ, cols]
        parts.append(s * (1.0 / w) - uu_scr[halo:halo + TQ, cols])
    d = jnp.concatenate(parts, axis=1)
    y_ref[0] = _merge_core(x_ref[0], d, gp_ref[0], ao_ref[0], ga_ref[0], gtp_ref[0], gta_ref[0],
                           pw_ref, ps_ref, wbp_ref, wba_ref, wo_ref, lng_ref, lnb_ref)


def _weight_specs(nd_grid):
    z = lambda nd: (lambda *a: (0,) * nd)
    gw = POOL_WIDTH // len(POOL_WINDOWS)
    return [pl.BlockSpec((len(POOL_WINDOWS), gw, gw), z(3)),
            pl.BlockSpec((1, POOL_WIDTH), z(2)),
            pl.BlockSpec((POOL_WIDTH, D_MODEL), z(2)),
            pl.BlockSpec((ATTN_WIDTH, D_MODEL), z(2)),
            pl.BlockSpec((D_MODEL, D_MODEL), z(2)),
            pl.BlockSpec((1, D_MODEL), z(2)),
            pl.BlockSpec((1, D_MODEL), z(2))]


def _merge_prompt(x, u, u_meta, gp, ao, ga, gtp, gta, weights):
    bsz, seq, _ = x.shape
    nt = seq // TQ
    hb = TQ // N_META
    tile = lambda n: pl.BlockSpec((1, TQ, n), lambda b, t: (b, t, 0))
    return pl.pallas_call(
        _merge_prompt_body,
        grid=(bsz, nt),
        in_specs=[tile(D_MODEL), tile(POOL_WIDTH),
                  pl.BlockSpec((1, N_META, POOL_WIDTH), lambda b, t: (b, jnp.maximum(t * hb - 1, 0), 0)),
                  pl.BlockSpec((N_META, POOL_WIDTH), lambda b, t: (0, 0)),
                  tile(POOL_WIDTH), tile(ATTN_WIDTH), tile(ATTN_WIDTH), tile(D_MODEL), tile(D_MODEL)]
                 + _weight_specs(2),
        out_specs=tile(D_MODEL),
        out_shape=jax.ShapeDtypeStruct((bsz, seq, D_MODEL), f32),
        scratch_shapes=[pltpu.VMEM((N_META + TQ, POOL_WIDTH), f32)],
        compiler_params=_cparams(2),
    )(x, u, u, u_meta, gp, ao, ga, gtp, gta, *weights)


def _merge_sample_body(x_ref, u_ref, sp_ref, gp_ref, ao_ref, ga_ref, gtp_ref, gta_ref,
                       pw_ref, ps_ref, wbp_ref, wba_ref, wo_ref, lng_ref, lnb_ref, y_ref, *, tnew):
    gw = POOL_WIDTH // len(POOL_WINDOWS)

    def hist(t, cols):
        if t < POOL_BUF:
            return sp_ref[:, POOL_WIDTH * t + cols.start:POOL_WIDTH * t + cols.stop]
        tt = t - POOL_BUF
        return u_ref[:, POOL_WIDTH * tt + cols.start:POOL_WIDTH * tt + cols.stop]

    for i in range(tnew):
        parts = []
        for g, w in enumerate(POOL_WINDOWS):
            cols = slice(gw * g, gw * (g + 1))
            s = hist(POOL_BUF + i, cols)
            for j in range(1, w):
                s = s + hist(POOL_BUF + i - j, cols)
            parts.append(s * (1.0 / w) - hist(POOL_BUF + i, cols))
        d = jnp.concatenate(parts, axis=1)
        c5 = slice(POOL_WIDTH * i, POOL_WIDTH * (i + 1))
        c10 = slice(D_MODEL * i, D_MODEL * (i + 1))
        y_ref[:, c10] = _merge_core(x_ref[:, c10], d, gp_ref[:, c5], ao_ref[:, c5], ga_ref[:, c5],
                                    gtp_ref[:, c10], gta_ref[:, c10],
                                    pw_ref, ps_ref, wbp_ref, wba_ref, wo_ref, lng_ref, lnb_ref)


def _merge_sample(x, u, state_pool, gp, ao, ga, gtp, gta, weights, tnew):
    dbsz = x.shape[0]
    full = lambda a: pl.BlockSpec(a.shape, lambda i: (0, 0))
    args = (x, u, state_pool, gp, ao, ga, gtp, gta)
    body = functools.partial(_merge_sample_body, tnew=tnew)
    return pl.pallas_call(
        body,
        grid=(1,),
        in_specs=[full(a) for a in args] + _weight_specs(1),
        out_specs=pl.BlockSpec((dbsz, tnew * D_MODEL), lambda i: (0, 0)),
        out_shape=jax.ShapeDtypeStruct((dbsz, tnew * D_MODEL), f32),
        compiler_params=_cparams(1),
    )(*args, *weights)


def _rel_bucket(n):
    max_exact = N_BUCKETS // 2
    nf = jnp.maximum(n, 1).astype(f32)
    large = max_exact + (jnp.log(nf / max_exact) / math.log(MAX_DISTANCE / max_exact)
                         * (N_BUCKETS - max_exact)).astype(i32)
    large = jnp.minimum(large, N_BUCKETS - 1)
    return jnp.where(n < max_exact, n, large)


def _bias_table(rel_bias, dist):
    bucket = _rel_bucket(jnp.maximum(dist, 0))
    rb = rel_bias.astype(f32)
    far = rb[N_BUCKETS - 1]
    shape = (N_HEADS,) + (1,) * dist.ndim
    out = jnp.zeros((N_HEADS,) + dist.shape, f32)
    for b in range(N_BUCKETS):
        out = jnp.where(bucket[None] == b, (rb[b] - far).reshape(shape), out)
    return out * LOG2E


def _split_w_in(w_in):
    splits = (512, 512, 512, 512, 512, 512, 256, 64, 4, 1024, 1024)
    offs = np.cumsum((0,) + splits)
    return {n: w_in[:, offs[i]:offs[i + 1]] for i, n in enumerate(
        ("u", "gp", "q", "k", "v", "ga", "qi", "ki", "wi", "gate_p", "gate_a"))}


def _pad_w_in(w_in):
    part = _split_w_in(w_in)
    cols = [part["u"], part["gp"], part["k"], part["ga"], part["ki"], part["ki"],
            part["gate_p"], part["gate_a"]]
    w = jnp.concatenate(cols, axis=1)
    assert w.shape[1] == W_COLS
    zrows = jnp.zeros((w_in.shape[0], BF16_ROWS - N_IDX_HEADS), w_in.dtype)
    wt = jnp.concatenate([part["k"], part["v"], part["q"], part["qi"] * (IDX_DIM ** -0.5), part["ki"],
                          part["wi"] * (N_IDX_HEADS ** -0.5), zrows], axis=1).T
    assert wt.shape[0] == WT_ROWS
    return w.astype(bf16), wt.astype(bf16)


def kernel(x_prompt, x_sample, cache_k, cache_v, cache_kidx, state_pool, page_table, meta, w_in,
           pool_w, pool_scale, w_br_pool, w_br_attn, rel_bias, w_out, ln_g, ln_b):
    bsz, seq, _ = x_prompt.shape
    dbsz, tnew, _ = x_sample.shape
    n_phys, page = cache_k.shape[0], cache_k.shape[1]
    n_pages = page_table.shape[1]
    past_len = n_pages * page
    assert (IDX_DIM ** -0.5, N_IDX_HEADS ** -0.5) == (0.125, 0.5)
    assert tnew <= SROWS and N_META <= LANES and tnew <= POOL_BUF

    w_pad, wt_pad = _pad_w_in(w_in)
    weights = (pool_w.astype(bf16), pool_scale.reshape(1, POOL_WIDTH).astype(f32),
               w_br_pool.astype(bf16), w_br_attn.astype(bf16), w_out.astype(bf16),
               ln_g.reshape(1, D_MODEL).astype(f32), ln_b.reshape(1, D_MODEL).astype(f32))

    (u, gp, kb, ga, kib, gtp, gta, kt, vt, vtb, qt, qit, kidxt, wit) = _project(
        x_prompt.reshape(bsz * seq, D_MODEL), w_pad, wt_pad, TQ)
    (u_m, _, kb_m, _, kib_m, _, _, kt_m, vt_m, vtb_m, _, _, kidxt_m, _) = _project(
        meta.astype(x_prompt.dtype), w_pad, wt_pad, N_META)
    r3 = lambda a: a.reshape(bsz, seq, a.shape[-1])
    pad_rows = lambda a: jnp.pad(a, ((0, META_TILE - N_META), (0, 0)))
    pad_cols = lambda a: jnp.pad(a, ((0, 0), (0, META_TILE - N_META)))

    ar = jnp.arange(TQ)
    assert TQ >= MAX_DISTANCE
    d_pp = jnp.stack([ar[None, :] - ar[:, None] + TQ * dd for dd in range(2)])
    bias_pp = jnp.swapaxes(_bias_table(rel_bias, d_pp), 0, 1)
    d_m = ar[None, :] + N_META - jnp.arange(META_TILE)[:, None]
    bias_meta = _bias_table(rel_bias, d_m)

    topk_p = min(TOPK_MAX, seq // 4)
    attn_o = _attn_prompt(qit, wit, qt, r3(kib), r3(kb), vtb,
                          pad_rows(kib_m), pad_rows(kb_m), pad_cols(vtb_m), bias_pp, bias_meta, topk_p)
    y_prompt = _merge_prompt(x_prompt, r3(u), u_m, r3(gp), attn_o, r3(ga), r3(gtp), r3(gta), weights)

    def with_meta_t(m_t, a_t):
        c = a_t.shape[0]
        a3 = jnp.swapaxes(a_t.reshape(c, bsz, seq), 0, 1)
        m3 = jnp.broadcast_to(m_t[None], (bsz, c, N_META))
        return jnp.swapaxes(jnp.concatenate([m3, a3], axis=2), 1, 2)

    k_prompt = with_meta_t(kt_m, kt).reshape(bsz, seq + N_META, N_HEADS, HEAD_DIM)
    v_prompt = with_meta_t(vt_m, vt).reshape(bsz, seq + N_META, N_HEADS, HEAD_DIM)
    kidx_prompt = with_meta_t(kidxt_m, kidxt)
    pool_prompt = r3(u)[:, seq - POOL_BUF:]

    (us, gps, _, gas, kibs, gtps, gtas, kts, vts, _, qts, qits, kidxts, wits) = _project(
        x_sample.reshape(dbsz * tnew, D_MODEL), w_pad, wt_pad, min(TQ, dbsz * tnew))
    s3 = lambda a: a.reshape(dbsz, tnew, a.shape[-1])
    t3 = lambda a_t: jnp.transpose(a_t.reshape(a_t.shape[0], dbsz, tnew), (1, 2, 0))
    padq = lambda a: jnp.pad(a, ((0, 0), (0, 0), (0, SROWS - tnew), (0, 0)))
    qi4 = jnp.swapaxes(t3(qits).reshape(dbsz, tnew, N_IDX_HEADS, IDX_DIM), 1, 2)
    qi32 = jnp.pad(padq(qi4), ((0, 0), (0, 0), (0, 0), (0, LANES - IDX_DIM)))
    qi32 = qi32.reshape(dbsz, N_IDX_HEADS * SROWS, LANES)
    wi4 = jnp.swapaxes(t3(wits)[:, :, :N_IDX_HEADS], 1, 2)[..., None]
    wb32 = jnp.broadcast_to(padq(wi4), (dbsz, N_IDX_HEADS, SROWS, LANES))
    wb32 = wb32.reshape(dbsz, N_IDX_HEADS * SROWS, LANES)
    kin2 = jnp.pad(s3(kibs), ((0, 0), (0, LANES - tnew), (0, 0)))
    kidx_t = jnp.swapaxes(cache_kidx, 1, 2)
    k_t = jnp.transpose(cache_k, (0, 2, 3, 1)).reshape(n_phys, ATTN_WIDTH, page)
    v_t = jnp.transpose(cache_v, (0, 2, 3, 1)).reshape(n_phys, ATTN_WIDTH, page)
    scores_past, scores_new = _sample_scores(page_table, qi32, wb32, kin2, kidx_t)

    topk_s = min(TOPK_MAX, (past_len + tnew) // 4)
    mask_past, mask_new = _sample_select(scores_past.reshape(dbsz * SROWS, past_len),
                                         scores_new.reshape(dbsz * SROWS, LANES), tnew, topk_s)

    head_of_lane = jnp.arange(ATTN_WIDTH) // HEAD_DIM
    q4 = jnp.where(head_of_lane[None, None, None, :] == jnp.arange(N_HEADS)[None, :, None, None],
                   t3(qts)[:, None, :, :], jnp.zeros((), bf16))
    q64 = padq(q4).reshape(dbsz, N_HEADS * SROWS, ATTN_WIDTH)
    new_t = lambda a_t: jnp.pad(jnp.swapaxes(a_t.reshape(ATTN_WIDTH, dbsz, tnew), 0, 1).astype(bf16),
                                ((0, 0), (0, 0), (0, LANES - tnew)))
    qrow = jnp.arange(SROWS)
    assert page >= MAX_DISTANCE
    d_s = (page + qrow)[:, None] - jnp.arange(LANES)[None, :]
    bias_s = _bias_table(rel_bias, d_s).reshape(N_HEADS * SROWS, LANES)
    d_n = qrow[:, None] - jnp.arange(LANES)[None, :]
    bias_n = _bias_table(rel_bias, d_n).reshape(N_HEADS * SROWS, LANES)
    ao_s = _sample_attend(page_table, q64,
                          mask_past.reshape(dbsz, SROWS, past_len), mask_new.reshape(dbsz, SROWS, LANES),
                          new_t(kts), new_t(vts), bias_s, bias_n, k_t, v_t)
    ao_s = ao_s[:, :tnew].reshape(dbsz, tnew * ATTN_WIDTH)

    flat = lambda a: a.reshape(dbsz, -1)
    y_sample = _merge_sample(flat(x_sample), flat(us), flat(state_pool.astype(f32)), flat(gps), ao_s,
                             flat(gas), flat(gtps), flat(gtas), weights, tnew)
    y_sample = y_sample.reshape(dbsz, tnew, D_MODEL)

    k_sample = t3(kts).reshape(dbsz, tnew, N_HEADS, HEAD_DIM)
    v_sample = t3(vts).reshape(dbsz, tnew, N_HEADS, HEAD_DIM)
    kidx_sample = t3(kidxts)
    pool_sample = jnp.concatenate([state_pool.astype(f32), s3(us)], axis=1)[:, -POOL_BUF:]

    return (y_prompt, y_sample, k_prompt, v_prompt, kidx_prompt, pool_prompt,
            k_sample, v_sample, kidx_sample, pool_sample)
```

```python
import functools
import math

import numpy as np
import jax
import jax.numpy as jnp
from jax import lax
from jax.experimental import pallas as pl
from jax.experimental.pallas import tpu as pltpu

f32 = jnp.float32
bf16 = jnp.bfloat16
i32 = jnp.int32

N_META = 16
POOL_WINDOWS = (2, 4, 8, 16)
POOL_BUF = max(POOL_WINDOWS) - 1
N_HEADS = 8
HEAD_DIM = 64
N_IDX_HEADS = 4
IDX_DIM = 64
TOPK_MAX = 256
N_BUCKETS = 32
MAX_DISTANCE = 128
DEPTH = 1
ALPHA = (2 * DEPTH) ** 0.25
LN_EPS = 1e-5
MASKED_LOGIT = -1e30

LANES = 128
SUBLANES = 8
BF16_ROWS = 16
WORD_BITS = 32
VMEM_LIMIT = 56 * 1024 * 1024

TQ = 256
GROUP_KEYS = WORD_BITS * LANES
INT_MIN = -(2 ** 31)
LOG2E = math.log2(math.e)
Q_SCALE = LOG2E * HEAD_DIM ** -0.5

D_MODEL = 1024
POOL_WIDTH = 512
ATTN_WIDTH = N_HEADS * HEAD_DIM
PAIR = 2 * HEAD_DIM

SEG = {
    "u": (0, 512), "gp": (512, 512), "k": (1024, 512), "ga": (1536, 512), "ki2": (2048, 128),
    "gate_p": (2176, 1024), "gate_a": (3200, 1024),
}
W_COLS = 4224
TSEG = {"kT": (0, 512), "vT": (512, 512), "qT": (1024, 512), "qiT": (1536, 256),
        "kidxT": (1792, IDX_DIM), "wiT": (1792 + IDX_DIM, BF16_ROWS)}
WT_ROWS = 1792 + IDX_DIM + BF16_ROWS
NT_DIMS = (((1,), (1,)), ((), ()))


def _cparams(n_axes):
    return pltpu.CompilerParams(dimension_semantics=("arbitrary",) * n_axes,
                                vmem_limit_bytes=VMEM_LIMIT)


def _project_body(x_ref, w_ref, wt_ref, u_o, gp_o, kb_o, ga_o, kib_o, gtp_o, gta_o,
                  kt_o, vt_o, vtb_o, qt_o, qit_o, kidxt_o, wit_o):
    xb = x_ref[...].astype(bf16)

    def seg(name):
        off, n = SEG[name]
        return jnp.dot(xb, w_ref[:, off:off + n], preferred_element_type=f32)

    def tseg(name):
        off, n = TSEG[name]
        return lax.dot_general(wt_ref[off:off + n, :], xb, NT_DIMS, preferred_element_type=f32)

    u_o[...] = seg("u")
    gp_o[...] = seg("gp")
    kb_o[...] = seg("k").astype(bf16)
    ga_o[...] = seg("ga")
    kib_o[...] = seg("ki2").astype(bf16)
    gtp_o[...] = seg("gate_p")
    gta_o[...] = seg("gate_a")
    kt_o[0] = tseg("kT")
    vt = tseg("vT")
    vt_o[0] = vt
    vtb_o[0] = vt.astype(bf16)
    qt_o[0] = (tseg("qT") * Q_SCALE).astype(bf16)
    qit_o[0] = tseg("qiT").astype(bf16)
    kidxt_o[0] = tseg("kidxT")
    wit_o[0] = tseg("wiT")[0:SUBLANES]


def _project(x2d, w_pad, wt_pad, tm, groups=1):
    rows = x2d.shape[0]
    assert rows % (groups * tm) == 0
    per_group = rows // (groups * tm)
    row_outs = [(512, f32), (512, f32), (512, bf16), (512, f32), (128, bf16), (1024, f32), (1024, f32)]
    col_outs = [(512, f32), (512, f32), (512, bf16), (512, bf16), (256, bf16), (IDX_DIM, f32),
                (SUBLANES, f32)]
    out_shape = ([jax.ShapeDtypeStruct((rows, n), dt) for n, dt in row_outs]
                 + [jax.ShapeDtypeStruct((groups, n, rows // groups), dt) for n, dt in col_outs])
    out_specs = ([pl.BlockSpec((tm, n), lambda i: (i, 0)) for n, _ in row_outs]
                 + [pl.BlockSpec((1, n, tm), lambda i: (i // per_group, 0, i % per_group))
                    for n, _ in col_outs])
    return pl.pallas_call(
        _project_body,
        grid=(rows // tm,),
        in_specs=[pl.BlockSpec((tm, D_MODEL), lambda i: (i, 0)),
                  pl.BlockSpec((D_MODEL, W_COLS), lambda i: (0, 0)),
                  pl.BlockSpec((WT_ROWS, D_MODEL), lambda i: (0, 0))],
        out_specs=out_specs,
        out_shape=out_shape,
        compiler_params=_cparams(1),
    )(x2d, w_pad, wt_pad)


def _score_key(score):
    bits = lax.bitcast_convert_type(score, i32)
    sign = lax.shift_right_arithmetic(bits, jnp.full(bits.shape, 31, i32))
    mag = bits & jnp.int32(0x7FFFFFFF)
    return (mag ^ sign) - sign


def _bit_transpose32(words):
    a = list(words)
    j = 16
    m = 0x0000FFFF
    while j != 0:
        k = 0
        sh = jnp.full(a[0].shape, j, i32)
        mm = jnp.int32(np.uint32(m).astype(np.int32))
        while k < 32:
            t = (a[k] ^ lax.shift_right_logical(a[k + j], sh)) & mm
            a[k] = a[k] ^ t
            a[k + j] = a[k + j] ^ lax.shift_left(t, sh)
            k = (k + j + 1) & ~j
        j >>= 1
        if j:
            m = (m ^ (m << j)) & 0xFFFFFFFF
    return a


def _bit_planes(words):
    t = _bit_transpose32(words[::-1])
    planes = t[::-1]
    planes[WORD_BITS - 1] = ~planes[WORD_BITS - 1]
    return planes


def _transpose_group(planes_scr, g, rows_total):
    def body(rg, carry):
        rows = pl.ds(pl.multiple_of(rg * SUBLANES, SUBLANES), SUBLANES)
        planes = _bit_planes([planes_scr[g, c, rows, :] for c in range(WORD_BITS)])
        for b in range(WORD_BITS):
            planes_scr[g, b, rows, :] = planes[b]
        return carry

    lax.fori_loop(0, rows_total // SUBLANES, body, 0)


def _lane_total(cnt, ones_mat):
    return jnp.dot(cnt.astype(f32).astype(bf16), ones_mat, preferred_element_type=f32)


def _radix_select(planes_scr, single_key, alive_scr, sel_scr, krem_scr, n_groups, single_first,
                  alive_init, k_row):
    rows = single_key.shape[0]
    nset = n_groups + 1
    ones_mat = jnp.ones((LANES, LANES), bf16)
    zero = jnp.zeros((rows, LANES), i32)

    for s in range(nset):
        alive_scr[s] = alive_init[s]
        sel_scr[s] = zero
    krem_scr[...] = k_row

    def step(planes):
        alive = [alive_scr[s] for s in range(nset)]
        ones = [alive[s] & planes[s] for s in range(nset)]
        cnt = ones[0]
        for s in range(1, nset):
            cnt = cnt + lax.population_count(ones[s])
        tot = _lane_total(cnt, ones_mat)
        krem = krem_scr[...]
        take = tot >= krem
        for s in range(nset):
            alive_scr[s] = jnp.where(take, ones[s], alive[s] ^ ones[s])
            sel_scr[s] = sel_scr[s] | jnp.where(take, zero, ones[s])
        krem_scr[...] = jnp.where(take, krem, krem - tot)

    def value_step(t, carry):
        b = WORD_BITS - 1 - t
        sh = jnp.full((rows, LANES), b, i32)
        planes = [lax.shift_right_logical(single_key, sh) & 1]
        for g in range(n_groups):
            planes.append(planes_scr[g, b])
        step(planes)
        return carry

    lax.fori_loop(0, WORD_BITS, value_step, 0)

    full = jnp.full((rows, LANES), -1, i32)
    one = jnp.full((rows, LANES), 1, i32)
    codes = [0] + [g + 1 for g in range(n_groups)] if single_first else [n_groups] + list(range(n_groups))
    n_code_bits = max(1, (max(codes)).bit_length())
    for cb in reversed(range(n_code_bits)):
        planes = []
        for s in range(nset):
            pref = ((codes[s] >> cb) & 1) == 0
            planes.append((one if s == 0 else full) if pref else zero)
        step(planes)
    for cm in CHUNK_MASKS:
        word = jnp.full((rows, LANES), np.uint32(cm).astype(np.int32), i32)
        step([one] + [word] * n_groups)
    lane = lax.broadcasted_iota(i32, (rows, LANES), 1)
    for lb in reversed(range(7)):
        low = ((lane >> lb) & 1) == 0
        lw = jnp.where(low, full, zero)
        step([lw & 1] + [lw] * n_groups)
    for s in range(nset):
        sel_scr[s] = sel_scr[s] | alive_scr[s]


CHUNK_MASKS = (0x0000FFFF, 0x00FF00FF, 0x0F0F0F0F, 0x33333333, 0x55555555)


def _mask_from_bits(word, bit):
    sh = jnp.full(word.shape, bit, i32)
    picked = (lax.shift_right_logical(word, sh) & 1) != 0
    return jnp.where(picked, 0.0, MASKED_LOGIT).astype(f32)


N_KG = TQ // SUBLANES
META_KG = N_META // SUBLANES
META_TILE = LANES
ONES_ROWS = BF16_ROWS
TILE_BUCKETS = (8, 16, 24, 32)


class _RefList:
    def __init__(self, refs):
        self.refs = list(refs)

    def _split(self, idx):
        idx = idx if isinstance(idx, tuple) else (idx,)
        return self.refs[idx[0]], (idx[1:] if len(idx) > 1 else (Ellipsis,))

    def __getitem__(self, idx):
        ref, rest = self._split(idx)
        return ref[rest]

    def __setitem__(self, idx, val):
        ref, rest = self._split(idx)
        ref[rest] = val


def _all_sublanes(x, op):
    return jnp.broadcast_to(op(x, axis=0, keepdims=True), x.shape)


def _select_prompt(planes_scr, keym_scr, alive_scr, sel_scr, alivem_scr, selm_scr, krem_scr,
                   n_tiles, i, k_row):
    tile_id = lax.broadcasted_iota(i32, (n_tiles, SUBLANES, TQ), 0)
    zero_t = jnp.zeros((n_tiles, SUBLANES, TQ), i32)
    full_t = jnp.full((n_tiles, SUBLANES, TQ), -1, i32)
    zero_m = jnp.zeros((META_KG, SUBLANES, TQ), i32)
    one_m = jnp.full((META_KG, SUBLANES, TQ), 1, i32)
    alive_scr[0:n_tiles] = jnp.where(tile_id <= i, full_t, zero_t)
    sel_scr[0:n_tiles] = zero_t
    alivem_scr[...] = one_m
    selm_scr[...] = zero_m
    krem_scr[...] = k_row

    def step(pm, pt):
        am = alivem_scr[...]
        at = alive_scr[0:n_tiles]
        om = am & pm
        ot = at & pt
        cnt = jnp.sum(lax.population_count(ot), axis=0) + jnp.sum(om, axis=0)
        tot = _all_sublanes(cnt.astype(f32), jnp.sum)
        krem = krem_scr[...]
        take = tot >= krem
        alive_scr[0:n_tiles] = jnp.where(take[None], ot, at ^ ot)
        sel_scr[0:n_tiles] = sel_scr[0:n_tiles] | jnp.where(take[None], zero_t, ot)
        alivem_scr[...] = jnp.where(take[None], om, am ^ om)
        selm_scr[...] = selm_scr[...] | jnp.where(take[None], zero_m, om)
        krem_scr[...] = jnp.where(take, krem, krem - tot)

    def value_step(t, carry):
        b = WORD_BITS - 1 - t
        sh = jnp.full((META_KG, SUBLANES, TQ), b, i32)
        step(lax.shift_right_logical(keym_scr[...], sh) & 1, planes_scr[0:n_tiles, b])
        return carry

    lax.fori_loop(0, WORD_BITS, value_step, 0)

    step(one_m, zero_t)
    for tb in reversed(range((n_tiles - 1).bit_length())):
        step(one_m, jnp.where(((tile_id >> tb) & 1) == 0, full_t, zero_t))
    for cm in CHUNK_MASKS:
        step(one_m, jnp.full((n_tiles, SUBLANES, TQ), np.uint32(cm).astype(np.int32), i32))
    kg_m = lax.broadcasted_iota(i32, (META_KG, SUBLANES, TQ), 0)
    for gb in reversed(range((META_KG - 1).bit_length())):
        step(jnp.where(((kg_m >> gb) & 1) == 0, one_m, zero_m), full_t)
    sub_t = lax.broadcasted_iota(i32, (n_tiles, SUBLANES, TQ), 1)
    sub_m = lax.broadcasted_iota(i32, (META_KG, SUBLANES, TQ), 1)
    for sb in reversed(range(3)):
        step(jnp.where(((sub_m >> sb) & 1) == 0, one_m, zero_m),
             jnp.where(((sub_t >> sb) & 1) == 0, full_t, zero_t))
    sel_scr[0:n_tiles] = sel_scr[0:n_tiles] | alive_scr[0:n_tiles]
    selm_scr[...] = selm_scr[...] | alivem_scr[...]


def _attn_prompt_body(qit_ref, wit_ref, qt_ref, ki_ref, k_ref, vt_ref, kim_ref, km_ref, vmt_ref,
                      bpp_ref, bmeta_ref, o_ref,
                      planes_scr, keym_scr, alive_scr, sel_scr, alivem_scr, selm_scr, krem_scr,
                      keyw_scr, qm_scr, madd_scr, ot_scr, *per_head, n_tiles, topk):
    m_scr, l_scr, acc_scr, s_scr, p_scr, alpha_scr = (
        _RefList(per_head[N_HEADS * g:N_HEADS * (g + 1)]) for g in range(6))
    i = pl.program_id(1)
    halves = [slice(LANES * qh, LANES * (qh + 1)) for qh in range(TQ // LANES)]
    zhalf = jnp.zeros((HEAD_DIM, TQ), bf16)

    qim = [jnp.concatenate([qit_ref[0, IDX_DIM * h:IDX_DIM * (h + 1), :], zhalf], axis=0)
           for h in range(N_IDX_HEADS)]
    wrow = [wit_ref[0, h:h + 1, :] for h in range(N_IDX_HEADS)]

    def score_block(kt2):
        acc = None
        for h in range(N_IDX_HEADS):
            s = jnp.dot(kt2, qim[h], preferred_element_type=f32)
            t = jnp.maximum(s, 0.0) * wrow[h]
            acc = t if acc is None else acc + t
        return acc

    def store_planes(t):
        for cols in halves:
            planes = _bit_planes([keyw_scr[SUBLANES * kg:SUBLANES * (kg + 1), cols]
                                  for kg in range(N_KG)])
            for b in range(WORD_BITS):
                planes_scr[t, b, :, cols] = planes[b]

    def key_rows(t):
        return pl.ds(pl.multiple_of(t * TQ, TQ), TQ)

    def score_tile(t, carry):
        keyw_scr[...] = _score_key(score_block(ki_ref[0, key_rows(t), :]))
        store_planes(t)
        return carry

    lax.fori_loop(0, i, score_tile, 0)
    krow = lax.broadcasted_iota(i32, (TQ, TQ), 0)
    qcol = lax.broadcasted_iota(i32, (TQ, TQ), 1)
    keyw_scr[...] = jnp.where(krow <= qcol, _score_key(score_block(ki_ref[0, key_rows(i), :])), INT_MIN)
    store_planes(i)
    keym_scr[...] = (_score_key(score_block(kim_ref[0:N_META, :])) ^ jnp.int32(INT_MIN)
                     ).reshape(META_KG, SUBLANES, TQ)

    qpos = lax.broadcasted_iota(i32, (SUBLANES, TQ), 1)
    k_row = jnp.minimum(i * TQ + qpos + (N_META + 1), topk).astype(f32)
    lo = 0
    for nb in TILE_BUCKETS:
        nb = min(nb, n_tiles)
        if nb <= lo:
            continue

        @pl.when((i >= lo) & (i < nb))
        def _(nb=nb):
            _select_prompt(planes_scr, keym_scr, alive_scr, sel_scr, alivem_scr, selm_scr, krem_scr,
                           nb, i, k_row)
        lo = nb
    assert lo == n_tiles

    for h in range(N_HEADS):
        qh = qt_ref[0, HEAD_DIM * h:HEAD_DIM * (h + 1), :]
        qm_scr[h] = jnp.concatenate([qh, zhalf] if h % 2 == 0 else [zhalf, qh], axis=0)
    for h in range(N_HEADS):
        m_scr[h] = jnp.full((SUBLANES, TQ), -jnp.inf, f32)
        l_scr[h] = jnp.zeros((SUBLANES, TQ), f32)
        acc_scr[h] = jnp.zeros((HEAD_DIM, TQ), f32)

    def attend(width, k_pair, vt_head, bias_fn):
        n_kg = width // SUBLANES
        ones_rows = jnp.ones((ONES_ROWS, width), bf16)
        for h in range(N_HEADS):
            s_scr[h, 0:width] = jnp.dot(k_pair(h // 2), qm_scr[h], preferred_element_type=f32)
        for h in range(N_HEADS):
            buf = h
            for cols in halves:
                vs = []
                for kg in range(n_kg):
                    rows = slice(SUBLANES * kg, SUBLANES * (kg + 1))
                    v = s_scr[buf, rows, cols] + madd_scr[rows, cols]
                    if bias_fn is not None:
                        v = v + bias_fn(h, rows, cols)
                    vs.append(v)
                mx = vs[0]
                for v in vs[1:]:
                    mx = jnp.maximum(mx, v)
                m_old = m_scr[h, :, cols]
                m_new = jnp.maximum(m_old, _all_sublanes(mx, jnp.max))
                m_scr[h, :, cols] = m_new
                alpha_scr[buf, :, cols] = jnp.exp2(m_old - m_new)
                for kk in range(n_kg // 2):
                    pp = jnp.concatenate([jnp.exp2(vs[2 * kk] - m_new),
                                          jnp.exp2(vs[2 * kk + 1] - m_new)], axis=0)
                    p_scr[buf, BF16_ROWS * kk:BF16_ROWS * (kk + 1), cols] = pp.astype(bf16)
        for h in range(N_HEADS):
            buf = h
            vaug = jnp.concatenate([vt_head(h), ones_rows], axis=0)
            ol = jnp.dot(vaug, p_scr[buf, 0:width], preferred_element_type=f32)
            a = alpha_scr[buf]
            for rg in range(HEAD_DIM // SUBLANES):
                rows = slice(SUBLANES * rg, SUBLANES * (rg + 1))
                acc_scr[h, rows] = a * acc_scr[h, rows] + ol[rows]
            l_scr[h] = a * l_scr[h] + ol[HEAD_DIM:HEAD_DIM + SUBLANES]

    def head_rows(h):
        return slice(HEAD_DIM * h, HEAD_DIM * (h + 1))

    def pair_cols(hp):
        return slice(PAIR * hp, PAIR * (hp + 1))

    for kg in range(META_TILE // SUBLANES):
        rows = slice(SUBLANES * kg, SUBLANES * (kg + 1))
        if kg < META_KG:
            madd_scr[rows] = _mask_from_bits(selm_scr[kg], 0)
        else:
            madd_scr[rows] = jnp.full((SUBLANES, TQ), MASKED_LOGIT, f32)
    meta_k = lambda hp: km_ref[:, pair_cols(hp)]
    meta_v = lambda h: vmt_ref[head_rows(h), :]

    @pl.when(i == 0)
    def _():
        attend(META_TILE, meta_k, meta_v, lambda h, rows, cols: bmeta_ref[h, rows, cols])

    @pl.when(i > 0)
    def _():
        attend(META_TILE, meta_k, meta_v, None)

    def set_mask(t):
        word = sel_scr[t]
        for kg in range(N_KG):
            madd_scr[SUBLANES * kg:SUBLANES * (kg + 1)] = _mask_from_bits(word, kg)

    def tile_k(t):
        return lambda hp: k_ref[0, key_rows(t), pair_cols(hp)]

    def tile_v(t):
        return lambda h: vt_ref[0, head_rows(h), key_rows(t)]

    def far_tile(t, carry):
        set_mask(t)
        attend(TQ, tile_k(t), tile_v(t), None)
        return carry

    lax.fori_loop(0, i - 1, far_tile, 0)

    @pl.when(i > 0)
    def _():
        set_mask(i - 1)
        attend(TQ, tile_k(i - 1), tile_v(i - 1), lambda h, rows, cols: bpp_ref[1, h, rows, cols])

    set_mask(i)
    attend(TQ, tile_k(i), tile_v(i), lambda h, rows, cols: bpp_ref[0, h, rows, cols])

    for h in range(N_HEADS):
        inv = 1.0 / l_scr[h]
        for rg in range(HEAD_DIM // SUBLANES):
            rows = slice(SUBLANES * rg, SUBLANES * (rg + 1))
            ot_scr[HEAD_DIM * h + SUBLANES * rg:HEAD_DIM * h + SUBLANES * (rg + 1)] = acc_scr[h, rows] * inv
    o_ref[0] = ot_scr[...].T


def _attn_prompt(qit, wit, qt, kib, kb, vtb, kim, km, vmt, bias_pp, bias_meta, topk):
    bsz, seq, _ = kb.shape
    assert seq % TQ == 0
    n_tiles = seq // TQ
    assert n_tiles <= TILE_BUCKETS[-1]
    const = lambda nd: (lambda b, i: (0,) * nd)
    body = functools.partial(_attn_prompt_body, n_tiles=n_tiles, topk=topk)
    one = pl.Buffered(1)
    per_head = lambda shape, dt: [pltpu.VMEM(shape, dt) for _ in range(N_HEADS)]
    return pl.pallas_call(
        body,
        grid=(bsz, n_tiles),
        in_specs=[
            pl.BlockSpec((1, N_IDX_HEADS * IDX_DIM, TQ), lambda b, i: (b, 0, i)),
            pl.BlockSpec((1, SUBLANES, TQ), lambda b, i: (b, 0, i)),
            pl.BlockSpec((1, ATTN_WIDTH, TQ), lambda b, i: (b, 0, i)),
            pl.BlockSpec((1, seq, LANES), lambda b, i: (b, 0, 0), pipeline_mode=one),
            pl.BlockSpec((1, seq, ATTN_WIDTH), lambda b, i: (b, 0, 0), pipeline_mode=one),
            pl.BlockSpec((1, ATTN_WIDTH, seq), lambda b, i: (b, 0, 0), pipeline_mode=one),
            pl.BlockSpec((META_TILE, LANES), const(2)),
            pl.BlockSpec((META_TILE, ATTN_WIDTH), const(2)),
            pl.BlockSpec((ATTN_WIDTH, META_TILE), const(2)),
            pl.BlockSpec((2, N_HEADS, TQ, TQ), const(4), pipeline_mode=one),
            pl.BlockSpec((N_HEADS, META_TILE, TQ), const(3), pipeline_mode=one),
        ],
        out_specs=pl.BlockSpec((1, TQ, ATTN_WIDTH), lambda b, i: (b, i, 0)),
        out_shape=jax.ShapeDtypeStruct((bsz, seq, ATTN_WIDTH), f32),
        scratch_shapes=[
            pltpu.VMEM((n_tiles, WORD_BITS, SUBLANES, TQ), i32),
            pltpu.VMEM((META_KG, SUBLANES, TQ), i32),
            pltpu.VMEM((n_tiles, SUBLANES, TQ), i32),
            pltpu.VMEM((n_tiles, SUBLANES, TQ), i32),
            pltpu.VMEM((META_KG, SUBLANES, TQ), i32),
            pltpu.VMEM((META_KG, SUBLANES, TQ), i32),
            pltpu.VMEM((SUBLANES, TQ), f32),
            pltpu.VMEM((TQ, TQ), i32),
            pltpu.VMEM((N_HEADS, PAIR, TQ), bf16),
            pltpu.VMEM((TQ, TQ), f32),
            pltpu.VMEM((ATTN_WIDTH, TQ), f32),
        ] + per_head((SUBLANES, TQ), f32)
          + per_head((SUBLANES, TQ), f32)
          + per_head((HEAD_DIM, TQ), f32)
          + per_head((TQ, TQ), f32)
          + per_head((TQ, TQ), bf16)
          + per_head((SUBLANES, TQ), f32),
        compiler_params=_cparams(2),
    )(qit, wit, qt, kib, kb, vtb, kim, km, vmt, bias_pp, bias_meta)


PAGES_PER_STEP = 16
IDX_PAGES_PER_STEP = 32
SROWS = 8
SAMPLE_SUB = 16


def _sample_scores_body(pt_ref, qi_ref, wb_ref, kin_ref, *rest):
    page_refs = rest[:IDX_PAGES_PER_STEP]
    out_ref, outn_ref = rest[IDX_PAGES_PER_STEP:]
    qi = qi_ref[0]
    wb = wb_ref[0]

    def combine(s):
        t = jnp.maximum(s, 0.0) * wb
        acc = t[0:SROWS]
        for h in range(1, N_IDX_HEADS):
            acc = acc + t[SROWS * h:SROWS * (h + 1)]
        return acc

    zpad = jnp.zeros((LANES - IDX_DIM, LANES), bf16)
    for jx in range(IDX_PAGES_PER_STEP):
        kt = jnp.concatenate([page_refs[jx][0].astype(bf16), zpad], axis=0)
        out_ref[0, :, LANES * jx:LANES * (jx + 1)] = combine(
            jnp.dot(qi, kt, preferred_element_type=f32))
    outn_ref[0] = combine(lax.dot_general(qi, kin_ref[0], NT_DIMS, preferred_element_type=f32))


def _sample_scores(page_table, qi32, wb32, kin2, kidx_t):
    dbsz, n_pages = page_table.shape
    page = kidx_t.shape[2]
    assert page == LANES and n_pages % IDX_PAGES_PER_STEP == 0
    nsteps = n_pages // IDX_PAGES_PER_STEP
    mrows = N_IDX_HEADS * SROWS

    def page_spec(jx):
        return pl.BlockSpec((1, IDX_DIM, page),
                            lambda b, s, pt: (pt[b, s * IDX_PAGES_PER_STEP + jx], 0, 0))

    grid_spec = pltpu.PrefetchScalarGridSpec(
        num_scalar_prefetch=1,
        grid=(dbsz, nsteps),
        in_specs=[pl.BlockSpec((1, mrows, LANES), lambda b, s, pt: (b, 0, 0)),
                  pl.BlockSpec((1, mrows, LANES), lambda b, s, pt: (b, 0, 0)),
                  pl.BlockSpec((1, LANES, LANES), lambda b, s, pt: (b, 0, 0))]
                 + [page_spec(jx) for jx in range(IDX_PAGES_PER_STEP)],
        out_specs=[pl.BlockSpec((1, SROWS, IDX_PAGES_PER_STEP * LANES), lambda b, s, pt: (b, 0, s)),
                   pl.BlockSpec((1, SROWS, LANES), lambda b, s, pt: (b, 0, 0))],
    )
    return pl.pallas_call(
        _sample_scores_body,
        grid_spec=grid_spec,
        out_shape=[jax.ShapeDtypeStruct((dbsz, SROWS, n_pages * page), f32),
                   jax.ShapeDtypeStruct((dbsz, SROWS, LANES), f32)],
        compiler_params=_cparams(2),
    )(page_table, qi32, wb32, kin2, *([kidx_t] * IDX_PAGES_PER_STEP))


SEL_ROWS = 128


def _sample_select_body(sp_ref, sn_ref, mp_ref, mn_ref, planes_scr, alive_scr, sel_scr, krem_scr,
                        *, n_groups, n_new, topk):
    n_chunks = sp_ref.shape[1] // LANES

    def key_chunk(c, carry):
        col = pl.ds(pl.multiple_of(c * LANES, LANES), LANES)
        planes_scr[c // WORD_BITS, c % WORD_BITS] = _score_key(sp_ref[:, col])
        return carry

    lax.fori_loop(0, n_chunks, key_chunk, 0)
    for c in range(n_chunks, n_groups * WORD_BITS):
        planes_scr[c // WORD_BITS, c % WORD_BITS] = jnp.full((SEL_ROWS, LANES), INT_MIN, i32)
    lane = lax.broadcasted_iota(i32, (SEL_ROWS, LANES), 1)
    row = lax.broadcasted_iota(i32, (SEL_ROWS, LANES), 0)
    causal_new = (lane <= (row & (SROWS - 1))) & (lane < n_new)
    key_n = jnp.where(causal_new, _score_key(sn_ref[...]), INT_MIN) ^ jnp.int32(INT_MIN)

    for g in range(n_groups):
        _transpose_group(planes_scr, g, SEL_ROWS)
    full = jnp.full((SEL_ROWS, LANES), -1, i32)
    alive_init = [jnp.full((SEL_ROWS, LANES), 1, i32)] + [full] * n_groups
    k_row = jnp.full((SEL_ROWS, LANES), topk, f32)
    _radix_select(planes_scr, key_n, alive_scr, sel_scr, krem_scr, n_groups, False, alive_init, k_row)

    def mask_chunk(c, carry):
        col = pl.ds(pl.multiple_of(c * LANES, LANES), LANES)
        mp_ref[:, col] = _mask_from_bits(sel_scr[1 + c // WORD_BITS], c % WORD_BITS)
        return carry

    lax.fori_loop(0, n_chunks, mask_chunk, 0)
    mn_ref[...] = _mask_from_bits(sel_scr[0], 0)


def _sample_select(scores_past, scores_new, n_new, topk):
    rows, past = scores_past.shape
    assert rows % SEL_ROWS == 0 and past % LANES == 0
    n_groups = -(-past // GROUP_KEYS)
    body = functools.partial(_sample_select_body, n_groups=n_groups, n_new=n_new, topk=topk)
    return pl.pallas_call(
        body,
        grid=(rows // SEL_ROWS,),
        in_specs=[pl.BlockSpec((SEL_ROWS, past), lambda r: (r, 0)),
                  pl.BlockSpec((SEL_ROWS, LANES), lambda r: (r, 0))],
        out_specs=[pl.BlockSpec((SEL_ROWS, past), lambda r: (r, 0)),
                   pl.BlockSpec((SEL_ROWS, LANES), lambda r: (r, 0))],
        out_shape=[jax.ShapeDtypeStruct((rows, past), f32),
                   jax.ShapeDtypeStruct((rows, LANES), f32)],
        scratch_shapes=[
            pltpu.VMEM((n_groups, WORD_BITS, SEL_ROWS, LANES), i32),
            pltpu.VMEM((n_groups + 1, SEL_ROWS, LANES), i32),
            pltpu.VMEM((n_groups + 1, SEL_ROWS, LANES), i32),
            pltpu.VMEM((SEL_ROWS, LANES), f32),
        ],
        compiler_params=_cparams(1),
    )(scores_past, scores_new)


def _sample_attend_body(pt_ref, q_ref, mp_ref, mn_ref, knt_ref, vnt_ref, bs_ref, bn_ref, *rest):
    k_refs = rest[:PAGES_PER_STEP]
    v_refs = rest[PAGES_PER_STEP:2 * PAGES_PER_STEP]
    o_ref, m_scr, l_scr, acc_scr, kcat_scr, vcat_scr, s_scr, p_scr, alpha_scr = rest[2 * PAGES_PER_STEP:]
    s_idx = pl.program_id(1)
    last_step = s_idx == pl.num_programs(1) - 1
    mrows = N_HEADS * SROWS
    q = q_ref[0]

    @pl.when(s_idx == 0)
    def _():
        m_scr[...] = jnp.full((mrows, LANES), -jnp.inf, f32)
        l_scr[...] = jnp.zeros((mrows, LANES), f32)
        acc_scr[...] = jnp.zeros((mrows, ATTN_WIDTH), f32)

    def attend(width, bias_tail, maskadd):
        s_scr[:, 0:width] = jnp.dot(q, kcat_scr[:, 0:width], preferred_element_type=f32)
        mk = jnp.concatenate([maskadd] * (SAMPLE_SUB // SROWS), axis=0)
        for r0 in range(0, mrows, SAMPLE_SUB):
            rows = slice(r0, r0 + SAMPLE_SUB)
            s = s_scr[rows, 0:width] + mk
            tail = s[:, width - LANES:] + bias_tail[rows]
            s = tail if width == LANES else jnp.concatenate([s[:, :width - LANES], tail], axis=1)
            m_old = m_scr[rows]
            m_new = jnp.maximum(m_old, jnp.max(s, axis=-1, keepdims=True))
            alpha = jnp.exp2(m_old - m_new)
            m_scr[rows] = m_new
            m_rep = m_new if width == LANES else jnp.concatenate([m_new] * (width // LANES), axis=1)
            p = jnp.exp2(s - m_rep)
            l_scr[rows] = alpha * l_scr[rows] + jnp.sum(p, axis=-1, keepdims=True)
            p_scr[rows, 0:width] = p.astype(bf16)
            alpha_scr[rows] = alpha
        o = lax.dot_general(p_scr[:, 0:width], vcat_scr[:, 0:width], NT_DIMS, preferred_element_type=f32)
        aw = jnp.concatenate([alpha_scr[...]] * (ATTN_WIDTH // LANES), axis=1)
        acc_scr[...] = aw * acc_scr[...] + o

    for jx in range(PAGES_PER_STEP):
        kcat_scr[:, LANES * jx:LANES * (jx + 1)] = k_refs[jx][0].astype(bf16)
        vcat_scr[:, LANES * jx:LANES * (jx + 1)] = v_refs[jx][0].astype(bf16)
    attend(PAGES_PER_STEP * LANES, bs_ref[...] * jnp.where(last_step, 1.0, 0.0), mp_ref[0])

    @pl.when(last_step)
    def _():
        kcat_scr[:, 0:LANES] = knt_ref[0]
        vcat_scr[:, 0:LANES] = vnt_ref[0]
        attend(LANES, bn_ref[...], mn_ref[0])
        l_w = jnp.concatenate([l_scr[...]] * (ATTN_WIDTH // LANES), axis=1)
        accn = acc_scr[...] / l_w
        head_of_lane = lax.broadcasted_iota(i32, (SROWS, ATTN_WIDTH), 1) // HEAD_DIM
        out = jnp.zeros((SROWS, ATTN_WIDTH), f32)
        for h in range(N_HEADS):
            out = out + jnp.where(head_of_lane == h, accn[SROWS * h:SROWS * (h + 1)], 0.0)
        o_ref[0] = out


def _sample_attend(page_table, q64, mask_past, mask_new, knt, vnt, bias_s, bias_n, k_t, v_t):
    dbsz, n_pages = page_table.shape
    page = k_t.shape[2]
    assert page == LANES and n_pages % PAGES_PER_STEP == 0
    nsteps = n_pages // PAGES_PER_STEP
    mrows = N_HEADS * SROWS
    width = PAGES_PER_STEP * LANES

    def page_spec(jx):
        return pl.BlockSpec((1, ATTN_WIDTH, page), lambda b, s, pt: (pt[b, s * PAGES_PER_STEP + jx], 0, 0))

    grid_spec = pltpu.PrefetchScalarGridSpec(
        num_scalar_prefetch=1,
        grid=(dbsz, nsteps),
        in_specs=[pl.BlockSpec((1, mrows, ATTN_WIDTH), lambda b, s, pt: (b, 0, 0)),
                  pl.BlockSpec((1, SROWS, width), lambda b, s, pt: (b, 0, s)),
                  pl.BlockSpec((1, SROWS, LANES), lambda b, s, pt: (b, 0, 0)),
                  pl.BlockSpec((1, ATTN_WIDTH, LANES), lambda b, s, pt: (b, 0, 0)),
                  pl.BlockSpec((1, ATTN_WIDTH, LANES), lambda b, s, pt: (b, 0, 0)),
                  pl.BlockSpec((mrows, LANES), lambda b, s, pt: (0, 0)),
                  pl.BlockSpec((mrows, LANES), lambda b, s, pt: (0, 0))]
                 + [page_spec(jx) for jx in range(PAGES_PER_STEP)]
                 + [page_spec(jx) for jx in range(PAGES_PER_STEP)],
        out_specs=pl.BlockSpec((1, SROWS, ATTN_WIDTH), lambda b, s, pt: (b, 0, 0)),
        scratch_shapes=[pltpu.VMEM((mrows, LANES), f32),
                        pltpu.VMEM((mrows, LANES), f32),
                        pltpu.VMEM((mrows, ATTN_WIDTH), f32),
                        pltpu.VMEM((ATTN_WIDTH, width), bf16),
                        pltpu.VMEM((ATTN_WIDTH, width), bf16),
                        pltpu.VMEM((mrows, width), f32),
                        pltpu.VMEM((mrows, width), bf16),
                        pltpu.VMEM((mrows, LANES), f32)],
    )
    return pl.pallas_call(
        _sample_attend_body,
        grid_spec=grid_spec,
        out_shape=jax.ShapeDtypeStruct((dbsz, SROWS, ATTN_WIDTH), f32),
        compiler_params=_cparams(2),
    )(page_table, q64, mask_past, mask_new, knt, vnt, bias_s, bias_n,
      *([k_t] * PAGES_PER_STEP), *([v_t] * PAGES_PER_STEP))


def _silu(x):
    return x * jax.nn.sigmoid(x)


def _merge_core(x, d, gp, ao, ga, gtp, gta, pw_ref, ps_ref, wbp_ref, wba_ref, wo_ref, lng_ref, lnb_ref):
    gw = POOL_WIDTH // len(POOL_WINDOWS)
    parts = [jnp.dot(d[:, gw * g:gw * (g + 1)].astype(bf16), pw_ref[g], preferred_element_type=f32)
             for g in range(len(POOL_WINDOWS))]
    pool_o = jnp.concatenate(parts, axis=1) * ps_ref[...]
    bp = jnp.dot((pool_o * _silu(gp)).astype(bf16), wbp_ref[...], preferred_element_type=f32)
    ba = jnp.dot((ao * _silu(ga)).astype(bf16), wba_ref[...], preferred_element_type=f32)
    m = jax.nn.sigmoid(gtp) * bp + jax.nn.sigmoid(gta) * ba
    out = jnp.dot(m.astype(bf16), wo_ref[...], preferred_element_type=f32)
    z = ALPHA * x + out
    mu = jnp.mean(z, axis=-1, keepdims=True)
    zc = z - mu
    var = jnp.mean(zc * zc, axis=-1, keepdims=True)
    return zc * lax.rsqrt(var + LN_EPS) * lng_ref[...] + lnb_ref[...]


def _merge_prompt_body(x_ref, u_ref, uh_ref, um_ref, gp_ref, ao_ref, ga_ref, gtp_ref, gta_ref,
                       pw_ref, ps_ref, wbp_ref, wba_ref, wo_ref, lng_ref, lnb_ref, y_ref, uu_scr):
    t = pl.program_id(1)
    halo = N_META
    uu_scr[0:halo, :] = jnp.where(t == 0, um_ref[...], uh_ref[0])
    uu_scr[halo:halo + TQ, :] = u_ref[0]
    gw = POOL_WIDTH // len(POOL_WINDOWS)
    parts = []
    for g, w in enumerate(POOL_WINDOWS):
        cols = slice(gw * g, gw * (g + 1))
        s = uu_scr[halo:halo + TQ, cols]
        for j in range(1, w):
            s = s + uu_scr[halo - j:halo - j + TQ, cols]
        parts.append(s * (1.0 / w) - uu_scr[halo:halo + TQ, cols])
    d = jnp.concatenate(parts, axis=1)
    y_ref[0] = _merge_core(x_ref[0], d, gp_ref[0], ao_ref[0], ga_ref[0], gtp_ref[0], gta_ref[0],
                           pw_ref, ps_ref, wbp_ref, wba_ref, wo_ref, lng_ref, lnb_ref)


def _weight_specs(nd_grid):
    z = lambda nd: (lambda *a: (0,) * nd)
    gw = POOL_WIDTH // len(POOL_WINDOWS)
    return [pl.BlockSpec((len(POOL_WINDOWS), gw, gw), z(3)),
            pl.BlockSpec((1, POOL_WIDTH), z(2)),
            pl.BlockSpec((POOL_WIDTH, D_MODEL), z(2)),
            pl.BlockSpec((ATTN_WIDTH, D_MODEL), z(2)),
            pl.BlockSpec((D_MODEL, D_MODEL), z(2)),
            pl.BlockSpec((1, D_MODEL), z(2)),
            pl.BlockSpec((1, D_MODEL), z(2))]


def _merge_prompt(x, u, u_meta, gp, ao, ga, gtp, gta, weights):
    bsz, seq, _ = x.shape
    nt = seq // TQ
    hb = TQ // N_META
    tile = lambda n: pl.BlockSpec((1, TQ, n), lambda b, t: (b, t, 0))
    return pl.pallas_call(
        _merge_prompt_body,
        grid=(bsz, nt),
        in_specs=[tile(D_MODEL), tile(POOL_WIDTH),
                  pl.BlockSpec((1, N_META, POOL_WIDTH), lambda b, t: (b, jnp.maximum(t * hb - 1, 0), 0)),
                  pl.BlockSpec((N_META, POOL_WIDTH), lambda b, t: (0, 0)),
                  tile(POOL_WIDTH), tile(ATTN_WIDTH), tile(ATTN_WIDTH), tile(D_MODEL), tile(D_MODEL)]
                 + _weight_specs(2),
        out_specs=tile(D_MODEL),
        out_shape=jax.ShapeDtypeStruct((bsz, seq, D_MODEL), f32),
        scratch_shapes=[pltpu.VMEM((N_META + TQ, POOL_WIDTH), f32)],
        compiler_params=_cparams(2),
    )(x, u, u, u_meta, gp, ao, ga, gtp, gta, *weights)


def _merge_sample_body(x_ref, u_ref, sp_ref, gp_ref, ao_ref, ga_ref, gtp_ref, gta_ref,
                       pw_ref, ps_ref, wbp_ref, wba_ref, wo_ref, lng_ref, lnb_ref, y_ref, *, tnew):
    gw = POOL_WIDTH // len(POOL_WINDOWS)

    def hist(t, cols):
        if t < POOL_BUF:
            return sp_ref[:, POOL_WIDTH * t + cols.start:POOL_WIDTH * t + cols.stop]
        tt = t - POOL_BUF
        return u_ref[:, POOL_WIDTH * tt + cols.start:POOL_WIDTH * tt + cols.stop]

    for i in range(tnew):
        parts = []
        for g, w in enumerate(POOL_WINDOWS):
            cols = slice(gw * g, gw * (g + 1))
            s = hist(POOL_BUF + i, cols)
            for j in range(1, w):
                s = s + hist(POOL_BUF + i - j, cols)
            parts.append(s * (1.0 / w) - hist(POOL_BUF + i, cols))
        d = jnp.concatenate(parts, axis=1)
        c5 = slice(POOL_WIDTH * i, POOL_WIDTH * (i + 1))
        c10 = slice(D_MODEL * i, D_MODEL * (i + 1))
        y_ref[:, c10] = _merge_core(x_ref[:, c10], d, gp_ref[:, c5], ao_ref[:, c5], ga_ref[:, c5],
                                    gtp_ref[:, c10], gta_ref[:, c10],
                                    pw_ref, ps_ref, wbp_ref, wba_ref, wo_ref, lng_ref, lnb_ref)


def _merge_sample(x, u, state_pool, gp, ao, ga, gtp, gta, weights, tnew):
    dbsz = x.shape[0]
    full = lambda a: pl.BlockSpec(a.shape, lambda i: (0, 0))
    args = (x, u, state_pool, gp, ao, ga, gtp, gta)
    body = functools.partial(_merge_sample_body, tnew=tnew)
    return pl.pallas_call(
        body,
        grid=(1,),
        in_specs=[full(a) for a in args] + _weight_specs(1),
        out_specs=pl.BlockSpec((dbsz, tnew * D_MODEL), lambda i: (0, 0)),
        out_shape=jax.ShapeDtypeStruct((dbsz, tnew * D_MODEL), f32),
        compiler_params=_cparams(1),
    )(*args, *weights)


def _rel_bucket(n):
    max_exact = N_BUCKETS // 2
    nf = jnp.maximum(n, 1).astype(f32)
    large = max_exact + (jnp.log(nf / max_exact) / math.log(MAX_DISTANCE / max_exact)
                         * (N_BUCKETS - max_exact)).astype(i32)
    large = jnp.minimum(large, N_BUCKETS - 1)
    return jnp.where(n < max_exact, n, large)


def _bias_table(rel_bias, dist):
    bucket = _rel_bucket(jnp.maximum(dist, 0))
    rb = rel_bias.astype(f32)
    far = rb[N_BUCKETS - 1]
    shape = (N_HEADS,) + (1,) * dist.ndim
    out = jnp.zeros((N_HEADS,) + dist.shape, f32)
    for b in range(N_BUCKETS):
        out = jnp.where(bucket[None] == b, (rb[b] - far).reshape(shape), out)
    return out * LOG2E


def _split_w_in(w_in):
    splits = (512, 512, 512, 512, 512, 512, 256, 64, 4, 1024, 1024)
    offs = np.cumsum((0,) + splits)
    return {n: w_in[:, offs[i]:offs[i + 1]] for i, n in enumerate(
        ("u", "gp", "q", "k", "v", "ga", "qi", "ki", "wi", "gate_p", "gate_a"))}


def _pad_w_in(w_in):
    part = _split_w_in(w_in)
    cols = [part["u"], part["gp"], part["k"], part["ga"], part["ki"], part["ki"],
            part["gate_p"], part["gate_a"]]
    w = jnp.concatenate(cols, axis=1)
    assert w.shape[1] == W_COLS
    zrows = jnp.zeros((w_in.shape[0], BF16_ROWS - N_IDX_HEADS), w_in.dtype)
    wt = jnp.concatenate([part["k"], part["v"], part["q"], part["qi"] * (IDX_DIM ** -0.5), part["ki"],
                          part["wi"] * (N_IDX_HEADS ** -0.5), zrows], axis=1).T
    assert wt.shape[0] == WT_ROWS
    return w.astype(bf16), wt.astype(bf16)


def kernel(x_prompt, x_sample, cache_k, cache_v, cache_kidx, state_pool, page_table, meta, w_in,
           pool_w, pool_scale, w_br_pool, w_br_attn, rel_bias, w_out, ln_g, ln_b):
    bsz, seq, _ = x_prompt.shape
    dbsz, tnew, _ = x_sample.shape
    n_phys, page = cache_k.shape[0], cache_k.shape[1]
    n_pages = page_table.shape[1]
    past_len = n_pages * page
    assert (IDX_DIM ** -0.5, N_IDX_HEADS ** -0.5) == (0.125, 0.5)
    assert tnew <= SROWS and N_META <= LANES and tnew <= POOL_BUF

    w_pad, wt_pad = _pad_w_in(w_in)
    weights = (pool_w.astype(bf16), pool_scale.reshape(1, POOL_WIDTH).astype(f32),
               w_br_pool.astype(bf16), w_br_attn.astype(bf16), w_out.astype(bf16),
               ln_g.reshape(1, D_MODEL).astype(f32), ln_b.reshape(1, D_MODEL).astype(f32))

    (u, gp, kb, ga, kib, gtp, gta, kt, vt, vtb, qt, qit, kidxt, wit) = _project(
        x_prompt.reshape(bsz * seq, D_MODEL), w_pad, wt_pad, TQ, groups=bsz)
    (u_m, _, kb_m, _, kib_m, _, _, kt_m, vt_m, vtb_m, _, _, kidxt_m, _) = _project(
        meta.astype(x_prompt.dtype), w_pad, wt_pad, N_META)
    r3 = lambda a: a.reshape(bsz, seq, a.shape[-1])
    pad_rows = lambda a: jnp.pad(a, ((0, META_TILE - N_META), (0, 0)))
    pad_cols = lambda a: jnp.pad(a, ((0, 0), (0, META_TILE - N_META)))

    ar = jnp.arange(TQ)
    assert TQ >= MAX_DISTANCE
    d_pp = jnp.stack([ar[None, :] - ar[:, None] + TQ * dd for dd in range(2)])
    bias_pp = jnp.swapaxes(_bias_table(rel_bias, d_pp), 0, 1)
    d_m = ar[None, :] + N_META - jnp.arange(META_TILE)[:, None]
    bias_meta = _bias_table(rel_bias, d_m)

    topk_p = min(TOPK_MAX, seq // 4)
    attn_o = _attn_prompt(qit, wit, qt, r3(kib), r3(kb), vtb,
                          pad_rows(kib_m), pad_rows(kb_m), pad_cols(vtb_m[0]), bias_pp, bias_meta, topk_p)
    y_prompt = _merge_prompt(x_prompt, r3(u), u_m, r3(gp), attn_o, r3(ga), r3(gtp), r3(gta), weights)

    def with_meta_t(m_t, a_t):
        m3 = jnp.broadcast_to(m_t, (bsz,) + m_t.shape[1:])
        return jnp.swapaxes(jnp.concatenate([m3, a_t], axis=2), 1, 2)

    k_prompt = with_meta_t(kt_m, kt).reshape(bsz, seq + N_META, N_HEADS, HEAD_DIM)
    v_prompt = with_meta_t(vt_m, vt).reshape(bsz, seq + N_META, N_HEADS, HEAD_DIM)
    kidx_prompt = with_meta_t(kidxt_m, kidxt)
    pool_prompt = r3(u)[:, seq - POOL_BUF:]

    (us, gps, _, gas, kibs, gtps, gtas, kts, vts, _, qts, qits, kidxts, wits) = _project(
        x_sample.reshape(dbsz * tnew, D_MODEL), w_pad, wt_pad, min(TQ, dbsz * tnew))
    s3 = lambda a: a.reshape(dbsz, tnew, a.shape[-1])
    t3 = lambda a_t: jnp.transpose(a_t.reshape(a_t.shape[1], dbsz, tnew), (1, 2, 0))
    padq = lambda a: jnp.pad(a, ((0, 0), (0, 0), (0, SROWS - tnew), (0, 0)))
    qi4 = jnp.swapaxes(t3(qits).reshape(dbsz, tnew, N_IDX_HEADS, IDX_DIM), 1, 2)
    qi32 = jnp.pad(padq(qi4), ((0, 0), (0, 0), (0, 0), (0, LANES - IDX_DIM)))
    qi32 = qi32.reshape(dbsz, N_IDX_HEADS * SROWS, LANES)
    wi4 = jnp.swapaxes(t3(wits)[:, :, :N_IDX_HEADS], 1, 2)[..., None]
    wb32 = jnp.broadcast_to(padq(wi4), (dbsz, N_IDX_HEADS, SROWS, LANES))
    wb32 = wb32.reshape(dbsz, N_IDX_HEADS * SROWS, LANES)
    kin2 = jnp.pad(s3(kibs), ((0, 0), (0, LANES - tnew), (0, 0)))
    kidx_t = jnp.swapaxes(cache_kidx, 1, 2)
    k_t = jnp.transpose(cache_k, (0, 2, 3, 1)).reshape(n_phys, ATTN_WIDTH, page)
    v_t = jnp.transpose(cache_v, (0, 2, 3, 1)).reshape(n_phys, ATTN_WIDTH, page)
    scores_past, scores_new = _sample_scores(page_table, qi32, wb32, kin2, kidx_t)

    topk_s = min(TOPK_MAX, (past_len + tnew) // 4)
    mask_past, mask_new = _sample_select(scores_past.reshape(dbsz * SROWS, past_len),
                                         scores_new.reshape(dbsz * SROWS, LANES), tnew, topk_s)

    head_of_lane = jnp.arange(ATTN_WIDTH) // HEAD_DIM
    q4 = jnp.where(head_of_lane[None, None, None, :] == jnp.arange(N_HEADS)[None, :, None, None],
                   t3(qts)[:, None, :, :], jnp.zeros((), bf16))
    q64 = padq(q4).reshape(dbsz, N_HEADS * SROWS, ATTN_WIDTH)
    new_t = lambda a_t: jnp.pad(jnp.swapaxes(a_t.reshape(ATTN_WIDTH, dbsz, tnew), 0, 1).astype(bf16),
                                ((0, 0), (0, 0), (0, LANES - tnew)))
    qrow = jnp.arange(SROWS)
    assert page >= MAX_DISTANCE
    d_s = (page + qrow)[:, None] - jnp.arange(LANES)[None, :]
    bias_s = _bias_table(rel_bias, d_s).reshape(N_HEADS * SROWS, LANES)
    d_n = qrow[:, None] - jnp.arange(LANES)[None, :]
    bias_n = _bias_table(rel_bias, d_n).reshape(N_HEADS * SROWS, LANES)
    ao_s = _sample_attend(page_table, q64,
                          mask_past.reshape(dbsz, SROWS, past_len), mask_new.reshape(dbsz, SROWS, LANES),
                          new_t(kts), new_t(vts), bias_s, bias_n, k_t, v_t)
    ao_s = ao_s[:, :tnew].reshape(dbsz, tnew * ATTN_WIDTH)

    flat = lambda a: a.reshape(dbsz, -1)
    y_sample = _merge_sample(flat(x_sample), flat(us), flat(state_pool.astype(f32)), flat(gps), ao_s,
                             flat(gas), flat(gtps), flat(gtas), weights, tnew)
    y_sample = y_sample.reshape(dbsz, tnew, D_MODEL)

    k_sample = t3(kts).reshape(dbsz, tnew, N_HEADS, HEAD_DIM)
    v_sample = t3(vts).reshape(dbsz, tnew, N_HEADS, HEAD_DIM)
    kidx_sample = t3(kidxts)
    pool_sample = jnp.concatenate([state_pool.astype(f32), s3(us)], axis=1)[:, -POOL_BUF:]

    return (y_prompt, y_sample, k_prompt, v_prompt, kidx_prompt, pool_prompt,
            k_sample, v_sample, kidx_sample, pool_sample)
```

```python
import functools
import math

import numpy as np
import jax
import jax.numpy as jnp
from jax import lax
from jax.experimental import pallas as pl
from jax.experimental.pallas import tpu as pltpu

f32 = jnp.float32
bf16 = jnp.bfloat16
i32 = jnp.int32

N_META = 16
POOL_WINDOWS = (2, 4, 8, 16)
POOL_BUF = max(POOL_WINDOWS) - 1
N_HEADS = 8
HEAD_DIM = 64
N_IDX_HEADS = 4
IDX_DIM = 64
TOPK_MAX = 256
N_BUCKETS = 32
MAX_DISTANCE = 128
DEPTH = 1
ALPHA = (2 * DEPTH) ** 0.25
LN_EPS = 1e-5
MASKED_LOGIT = -1e30

LANES = 128
SUBLANES = 8
BF16_ROWS = 16
WORD_BITS = 32
VMEM_LIMIT = 56 * 1024 * 1024

TQ = 256
GROUP_KEYS = WORD_BITS * LANES
INT_MIN = -(2 ** 31)
LOG2E = math.log2(math.e)
Q_SCALE = LOG2E * HEAD_DIM ** -0.5

D_MODEL = 1024
POOL_WIDTH = 512
ATTN_WIDTH = N_HEADS * HEAD_DIM
PAIR = 2 * HEAD_DIM

SEG = {
    "u": (0, 512), "gp": (512, 512), "k": (1024, 512), "ga": (1536, 512), "ki2": (2048, 128),
    "gate_p": (2176, 1024), "gate_a": (3200, 1024),
}
W_COLS = 4224
TSEG = {"kT": (0, 512), "vT": (512, 512), "qT": (1024, 512), "qiT": (1536, 256),
        "kidxT": (1792, IDX_DIM), "wiT": (1792 + IDX_DIM, BF16_ROWS)}
WT_ROWS = 1792 + IDX_DIM + BF16_ROWS
NT_DIMS = (((1,), (1,)), ((), ()))


def _cparams(n_axes):
    return pltpu.CompilerParams(dimension_semantics=("arbitrary",) * n_axes,
                                vmem_limit_bytes=VMEM_LIMIT)


def _project_body(x_ref, w_ref, wt_ref, u_o, gp_o, kb_o, ga_o, kib_o, gtp_o, gta_o,
                  kt_o, vt_o, vtb_o, qt_o, qit_o, kidxt_o, wit_o):
    xb = x_ref[...].astype(bf16)

    def seg(name):
        off, n = SEG[name]
        return jnp.dot(xb, w_ref[:, off:off + n], preferred_element_type=f32)

    def tseg(name):
        off, n = TSEG[name]
        return lax.dot_general(wt_ref[off:off + n, :], xb, NT_DIMS, preferred_element_type=f32)

    u_o[...] = seg("u")
    gp_o[...] = seg("gp")
    kb_o[...] = seg("k").astype(bf16)
    ga_o[...] = seg("ga")
    kib_o[...] = seg("ki2").astype(bf16)
    gtp_o[...] = seg("gate_p")
    gta_o[...] = seg("gate_a")
    kt_o[0] = tseg("kT")
    vt = tseg("vT")
    vt_o[0] = vt
    vtb_o[0] = vt.astype(bf16)
    qt_o[0] = (tseg("qT") * Q_SCALE).astype(bf16)
    qit_o[0] = tseg("qiT").astype(bf16)
    kidxt_o[0] = tseg("kidxT")
    wit_o[0] = tseg("wiT")[0:SUBLANES]


def _project(x2d, w_pad, wt_pad, tm, groups=1):
    rows = x2d.shape[0]
    assert rows % (groups * tm) == 0
    per_group = rows // (groups * tm)
    row_outs = [(512, f32), (512, f32), (512, bf16), (512, f32), (128, bf16), (1024, f32), (1024, f32)]
    col_outs = [(512, f32), (512, f32), (512, bf16), (512, bf16), (256, bf16), (IDX_DIM, f32),
                (SUBLANES, f32)]
    out_shape = ([jax.ShapeDtypeStruct((rows, n), dt) for n, dt in row_outs]
                 + [jax.ShapeDtypeStruct((groups, n, rows // groups), dt) for n, dt in col_outs])
    out_specs = ([pl.BlockSpec((tm, n), lambda i: (i, 0)) for n, _ in row_outs]
                 + [pl.BlockSpec((1, n, tm), lambda i: (i // per_group, 0, i % per_group))
                    for n, _ in col_outs])
    return pl.pallas_call(
        _project_body,
        grid=(rows // tm,),
        in_specs=[pl.BlockSpec((tm, D_MODEL), lambda i: (i, 0)),
                  pl.BlockSpec((D_MODEL, W_COLS), lambda i: (0, 0)),
                  pl.BlockSpec((WT_ROWS, D_MODEL), lambda i: (0, 0))],
        out_specs=out_specs,
        out_shape=out_shape,
        compiler_params=_cparams(1),
    )(x2d, w_pad, wt_pad)


def _score_key(score):
    bits = lax.bitcast_convert_type(score, i32)
    sign = lax.shift_right_arithmetic(bits, jnp.full(bits.shape, 31, i32))
    mag = bits & jnp.int32(0x7FFFFFFF)
    return (mag ^ sign) - sign


def _bit_transpose32(words):
    a = list(words)
    j = 16
    m = 0x0000FFFF
    while j != 0:
        k = 0
        sh = jnp.full(a[0].shape, j, i32)
        mm = jnp.int32(np.uint32(m).astype(np.int32))
        while k < 32:
            t = (a[k] ^ lax.shift_right_logical(a[k + j], sh)) & mm
            a[k] = a[k] ^ t
            a[k + j] = a[k + j] ^ lax.shift_left(t, sh)
            k = (k + j + 1) & ~j
        j >>= 1
        if j:
            m = (m ^ (m << j)) & 0xFFFFFFFF
    return a


def _bit_planes(words):
    t = _bit_transpose32(words[::-1])
    planes = t[::-1]
    planes[WORD_BITS - 1] = ~planes[WORD_BITS - 1]
    return planes


def _transpose_group(planes_scr, g, rows_total):
    def body(rg, carry):
        rows = pl.ds(pl.multiple_of(rg * SUBLANES, SUBLANES), SUBLANES)
        planes = _bit_planes([planes_scr[g, c, rows, :] for c in range(WORD_BITS)])
        for b in range(WORD_BITS):
            planes_scr[g, b, rows, :] = planes[b]
        return carry

    lax.fori_loop(0, rows_total // SUBLANES, body, 0)


def _lane_total(cnt, ones_mat):
    return jnp.dot(cnt.astype(f32).astype(bf16), ones_mat, preferred_element_type=f32)


def _radix_select(planes_scr, single_key, alive_scr, sel_scr, krem_scr, n_groups, single_first,
                  alive_init, k_row):
    rows = single_key.shape[0]
    nset = n_groups + 1
    ones_mat = jnp.ones((LANES, LANES), bf16)
    zero = jnp.zeros((rows, LANES), i32)

    for s in range(nset):
        alive_scr[s] = alive_init[s]
        sel_scr[s] = zero
    krem_scr[...] = k_row

    def step(planes):
        alive = [alive_scr[s] for s in range(nset)]
        ones = [alive[s] & planes[s] for s in range(nset)]
        cnt = ones[0]
        for s in range(1, nset):
            cnt = cnt + lax.population_count(ones[s])
        tot = _lane_total(cnt, ones_mat)
        krem = krem_scr[...]
        take = tot >= krem
        for s in range(nset):
            alive_scr[s] = jnp.where(take, ones[s], alive[s] ^ ones[s])
            sel_scr[s] = sel_scr[s] | jnp.where(take, zero, ones[s])
        krem_scr[...] = jnp.where(take, krem, krem - tot)

    def value_step(t, carry):
        b = WORD_BITS - 1 - t
        sh = jnp.full((rows, LANES), b, i32)
        planes = [lax.shift_right_logical(single_key, sh) & 1]
        for g in range(n_groups):
            planes.append(planes_scr[g, b])
        step(planes)
        return carry

    lax.fori_loop(0, WORD_BITS, value_step, 0)

    full = jnp.full((rows, LANES), -1, i32)
    one = jnp.full((rows, LANES), 1, i32)
    codes = [0] + [g + 1 for g in range(n_groups)] if single_first else [n_groups] + list(range(n_groups))
    n_code_bits = max(1, (max(codes)).bit_length())
    for cb in reversed(range(n_code_bits)):
        planes = []
        for s in range(nset):
            pref = ((codes[s] >> cb) & 1) == 0
            planes.append((one if s == 0 else full) if pref else zero)
        step(planes)
    for cm in CHUNK_MASKS:
        word = jnp.full((rows, LANES), np.uint32(cm).astype(np.int32), i32)
        step([one] + [word] * n_groups)
    lane = lax.broadcasted_iota(i32, (rows, LANES), 1)
    for lb in reversed(range(7)):
        low = ((lane >> lb) & 1) == 0
        lw = jnp.where(low, full, zero)
        step([lw & 1] + [lw] * n_groups)
    for s in range(nset):
        sel_scr[s] = sel_scr[s] | alive_scr[s]


CHUNK_MASKS = (0x0000FFFF, 0x00FF00FF, 0x0F0F0F0F, 0x33333333, 0x55555555)


def _mask_from_bits(word, bit):
    sh = jnp.full(word.shape, bit, i32)
    picked = (lax.shift_right_logical(word, sh) & 1) != 0
    return jnp.where(picked, 0.0, MASKED_LOGIT).astype(f32)


N_KG = TQ // SUBLANES
META_KG = N_META // SUBLANES
META_TILE = LANES
ONES_ROWS = BF16_ROWS
TILE_BUCKETS = (8, 16, 24, 32)


class _RefList:
    def __init__(self, refs):
        self.refs = list(refs)

    def _split(self, idx):
        idx = idx if isinstance(idx, tuple) else (idx,)
        return self.refs[idx[0]], (idx[1:] if len(idx) > 1 else (Ellipsis,))

    def __getitem__(self, idx):
        ref, rest = self._split(idx)
        return ref[rest]

    def __setitem__(self, idx, val):
        ref, rest = self._split(idx)
        ref[rest] = val


def _all_sublanes(x, op):
    return jnp.broadcast_to(op(x, axis=0, keepdims=True), x.shape)


def _select_prompt(planes_scr, keym_scr, alive_scr, sel_scr, alivem_scr, selm_scr, krem_scr,
                   n_tiles, i, k_row):
    tile_id = lax.broadcasted_iota(i32, (n_tiles, SUBLANES, TQ), 0)
    zero_t = jnp.zeros((n_tiles, SUBLANES, TQ), i32)
    full_t = jnp.full((n_tiles, SUBLANES, TQ), -1, i32)
    zero_m = jnp.zeros((META_KG, SUBLANES, TQ), i32)
    one_m = jnp.full((META_KG, SUBLANES, TQ), 1, i32)
    alive_scr[0:n_tiles] = jnp.where(tile_id <= i, full_t, zero_t)
    sel_scr[0:n_tiles] = zero_t
    alivem_scr[...] = one_m
    selm_scr[...] = zero_m
    krem_scr[...] = k_row

    def step(pm, pt):
        am = alivem_scr[...]
        at = alive_scr[0:n_tiles]
        om = am & pm
        ot = at & pt
        cnt = jnp.sum(lax.population_count(ot), axis=0) + jnp.sum(om, axis=0)
        tot = _all_sublanes(cnt.astype(f32), jnp.sum)
        krem = krem_scr[...]
        take = tot >= krem
        alive_scr[0:n_tiles] = jnp.where(take[None], ot, at ^ ot)
        sel_scr[0:n_tiles] = sel_scr[0:n_tiles] | jnp.where(take[None], zero_t, ot)
        alivem_scr[...] = jnp.where(take[None], om, am ^ om)
        selm_scr[...] = selm_scr[...] | jnp.where(take[None], zero_m, om)
        krem_scr[...] = jnp.where(take, krem, krem - tot)

    def value_step(t, carry):
        b = WORD_BITS - 1 - t
        sh = jnp.full((META_KG, SUBLANES, TQ), b, i32)
        step(lax.shift_right_logical(keym_scr[...], sh) & 1, planes_scr[0:n_tiles, b])
        return carry

    lax.fori_loop(0, WORD_BITS, value_step, 0)

    step(one_m, zero_t)
    for tb in reversed(range((n_tiles - 1).bit_length())):
        step(one_m, jnp.where(((tile_id >> tb) & 1) == 0, full_t, zero_t))
    for cm in CHUNK_MASKS:
        step(one_m, jnp.full((n_tiles, SUBLANES, TQ), np.uint32(cm).astype(np.int32), i32))
    kg_m = lax.broadcasted_iota(i32, (META_KG, SUBLANES, TQ), 0)
    for gb in reversed(range((META_KG - 1).bit_length())):
        step(jnp.where(((kg_m >> gb) & 1) == 0, one_m, zero_m), full_t)
    sub_t = lax.broadcasted_iota(i32, (n_tiles, SUBLANES, TQ), 1)
    sub_m = lax.broadcasted_iota(i32, (META_KG, SUBLANES, TQ), 1)
    for sb in reversed(range(3)):
        step(jnp.where(((sub_m >> sb) & 1) == 0, one_m, zero_m),
             jnp.where(((sub_t >> sb) & 1) == 0, full_t, zero_t))
    sel_scr[0:n_tiles] = sel_scr[0:n_tiles] | alive_scr[0:n_tiles]
    selm_scr[...] = selm_scr[...] | alivem_scr[...]


def _attn_prompt_body(qit_ref, wit_ref, qt_ref, ki_ref, k_ref, vt_ref, kim_ref, km_ref, vmt_ref,
                      bpp_ref, bmeta_ref, o_ref,
                      planes_scr, keym_scr, alive_scr, sel_scr, alivem_scr, selm_scr, krem_scr,
                      keyw_scr, qm_scr, madd_scr, ot_scr, *per_head, n_tiles, topk):
    m_scr, l_scr, acc_scr, s_scr, p_scr, alpha_scr = (
        _RefList(per_head[N_HEADS * g:N_HEADS * (g + 1)]) for g in range(6))
    i = pl.program_id(1)
    halves = [slice(LANES * qh, LANES * (qh + 1)) for qh in range(TQ // LANES)]
    zhalf = jnp.zeros((HEAD_DIM, TQ), bf16)

    qim = [jnp.concatenate([qit_ref[0, IDX_DIM * h:IDX_DIM * (h + 1), :], zhalf], axis=0)
           for h in range(N_IDX_HEADS)]
    wrow = [wit_ref[0, h:h + 1, :] for h in range(N_IDX_HEADS)]

    def score_block(kt2):
        acc = None
        for h in range(N_IDX_HEADS):
            s = jnp.dot(kt2, qim[h], preferred_element_type=f32)
            t = jnp.maximum(s, 0.0) * wrow[h]
            acc = t if acc is None else acc + t
        return acc

    def store_planes(t, buf):
        for cols in halves:
            planes = _bit_planes([keyw_scr[buf, SUBLANES * kg:SUBLANES * (kg + 1), cols]
                                  for kg in range(N_KG)])
            for b in range(WORD_BITS):
                planes_scr[t, b, :, cols] = planes[b]

    def key_rows(t):
        return pl.ds(pl.multiple_of(t * TQ, TQ), TQ)

    def tile_keys(t):
        return _score_key(score_block(ki_ref[0, key_rows(t), :]))

    def score_pair(tt, carry):
        keyw_scr[0] = tile_keys(2 * tt)
        keyw_scr[1] = tile_keys(2 * tt + 1)
        store_planes(2 * tt, 0)
        store_planes(2 * tt + 1, 1)
        return carry

    lax.fori_loop(0, i // 2, score_pair, 0)

    @pl.when(i % 2 == 1)
    def _():
        keyw_scr[0] = tile_keys(i - 1)
        store_planes(i - 1, 0)

    krow = lax.broadcasted_iota(i32, (TQ, TQ), 0)
    qcol = lax.broadcasted_iota(i32, (TQ, TQ), 1)
    keyw_scr[1] = jnp.where(krow <= qcol, tile_keys(i), INT_MIN)
    store_planes(i, 1)
    keym_scr[...] = (_score_key(score_block(kim_ref[0:N_META, :])) ^ jnp.int32(INT_MIN)
                     ).reshape(META_KG, SUBLANES, TQ)

    qpos = lax.broadcasted_iota(i32, (SUBLANES, TQ), 1)
    k_row = jnp.minimum(i * TQ + qpos + (N_META + 1), topk).astype(f32)
    lo = 0
    for nb in TILE_BUCKETS:
        nb = min(nb, n_tiles)
        if nb <= lo:
            continue

        @pl.when((i >= lo) & (i < nb))
        def _(nb=nb):
            _select_prompt(planes_scr, keym_scr, alive_scr, sel_scr, alivem_scr, selm_scr, krem_scr,
                           nb, i, k_row)
        lo = nb
    assert lo == n_tiles

    for h in range(N_HEADS):
        qh = qt_ref[0, HEAD_DIM * h:HEAD_DIM * (h + 1), :]
        qm_scr[h] = jnp.concatenate([qh, zhalf] if h % 2 == 0 else [zhalf, qh], axis=0)
    for h in range(N_HEADS):
        m_scr[h] = jnp.full((SUBLANES, TQ), -jnp.inf, f32)
        l_scr[h] = jnp.zeros((SUBLANES, TQ), f32)
        acc_scr[h] = jnp.zeros((HEAD_DIM, TQ), f32)

    def attend(width, k_pair, vt_head, bias_fn):
        n_kg = width // SUBLANES
        ones_rows = jnp.ones((ONES_ROWS, width), bf16)
        for h in range(N_HEADS):
            s_scr[h, 0:width] = jnp.dot(k_pair(h // 2), qm_scr[h], preferred_element_type=f32)
        for h in range(N_HEADS):
            buf = h
            for cols in halves:
                vs = []
                for kg in range(n_kg):
                    rows = slice(SUBLANES * kg, SUBLANES * (kg + 1))
                    v = s_scr[buf, rows, cols] + madd_scr[rows, cols]
                    if bias_fn is not None:
                        v = v + bias_fn(h, rows, cols)
                    vs.append(v)
                mx = vs[0]
                for v in vs[1:]:
                    mx = jnp.maximum(mx, v)
                m_old = m_scr[h, :, cols]
                m_new = jnp.maximum(m_old, _all_sublanes(mx, jnp.max))
                m_scr[h, :, cols] = m_new
                alpha_scr[buf, :, cols] = jnp.exp2(m_old - m_new)
                for kk in range(n_kg // 2):
                    pp = jnp.concatenate([jnp.exp2(vs[2 * kk] - m_new),
                                          jnp.exp2(vs[2 * kk + 1] - m_new)], axis=0)
                    p_scr[buf, BF16_ROWS * kk:BF16_ROWS * (kk + 1), cols] = pp.astype(bf16)
        for h in range(N_HEADS):
            buf = h
            vaug = jnp.concatenate([vt_head(h), ones_rows], axis=0)
            ol = jnp.dot(vaug, p_scr[buf, 0:width], preferred_element_type=f32)
            a = alpha_scr[buf]
            for rg in range(HEAD_DIM // SUBLANES):
                rows = slice(SUBLANES * rg, SUBLANES * (rg + 1))
                acc_scr[h, rows] = a * acc_scr[h, rows] + ol[rows]
            l_scr[h] = a * l_scr[h] + ol[HEAD_DIM:HEAD_DIM + SUBLANES]

    def head_rows(h):
        return slice(HEAD_DIM * h, HEAD_DIM * (h + 1))

    def pair_cols(hp):
        return slice(PAIR * hp, PAIR * (hp + 1))

    for kg in range(META_TILE // SUBLANES):
        rows = slice(SUBLANES * kg, SUBLANES * (kg + 1))
        if kg < META_KG:
            madd_scr[rows] = _mask_from_bits(selm_scr[kg], 0)
        else:
            madd_scr[rows] = jnp.full((SUBLANES, TQ), MASKED_LOGIT, f32)
    meta_k = lambda hp: km_ref[:, pair_cols(hp)]
    meta_v = lambda h: vmt_ref[head_rows(h), :]

    @pl.when(i == 0)
    def _():
        attend(META_TILE, meta_k, meta_v, lambda h, rows, cols: bmeta_ref[h, rows, cols])

    @pl.when(i > 0)
    def _():
        attend(META_TILE, meta_k, meta_v, None)

    def set_mask(t):
        word = sel_scr[t]
        for kg in range(N_KG):
            madd_scr[SUBLANES * kg:SUBLANES * (kg + 1)] = _mask_from_bits(word, kg)

    def tile_k(t):
        return lambda hp: k_ref[0, key_rows(t), pair_cols(hp)]

    def tile_v(t):
        return lambda h: vt_ref[0, head_rows(h), key_rows(t)]

    def far_tile(t, carry):
        set_mask(t)
        attend(TQ, tile_k(t), tile_v(t), None)
        return carry

    lax.fori_loop(0, i - 1, far_tile, 0)

    @pl.when(i > 0)
    def _():
        set_mask(i - 1)
        attend(TQ, tile_k(i - 1), tile_v(i - 1), lambda h, rows, cols: bpp_ref[1, h, rows, cols])

    set_mask(i)
    attend(TQ, tile_k(i), tile_v(i), lambda h, rows, cols: bpp_ref[0, h, rows, cols])

    for h in range(N_HEADS):
        inv = 1.0 / l_scr[h]
        for rg in range(HEAD_DIM // SUBLANES):
            rows = slice(SUBLANES * rg, SUBLANES * (rg + 1))
            ot_scr[HEAD_DIM * h + SUBLANES * rg:HEAD_DIM * h + SUBLANES * (rg + 1)] = acc_scr[h, rows] * inv
    o_ref[0] = ot_scr[...].T


def _attn_prompt(qit, wit, qt, kib, kb, vtb, kim, km, vmt, bias_pp, bias_meta, topk):
    bsz, seq, _ = kb.shape
    assert seq % TQ == 0
    n_tiles = seq // TQ
    assert n_tiles <= TILE_BUCKETS[-1]
    const = lambda nd: (lambda b, i: (0,) * nd)
    body = functools.partial(_attn_prompt_body, n_tiles=n_tiles, topk=topk)
    one = pl.Buffered(1)
    per_head = lambda shape, dt: [pltpu.VMEM(shape, dt) for _ in range(N_HEADS)]
    return pl.pallas_call(
        body,
        grid=(bsz, n_tiles),
        in_specs=[
            pl.BlockSpec((1, N_IDX_HEADS * IDX_DIM, TQ), lambda b, i: (b, 0, i)),
            pl.BlockSpec((1, SUBLANES, TQ), lambda b, i: (b, 0, i)),
            pl.BlockSpec((1, ATTN_WIDTH, TQ), lambda b, i: (b, 0, i)),
            pl.BlockSpec((1, seq, LANES), lambda b, i: (b, 0, 0), pipeline_mode=one),
            pl.BlockSpec((1, seq, ATTN_WIDTH), lambda b, i: (b, 0, 0), pipeline_mode=one),
            pl.BlockSpec((1, ATTN_WIDTH, seq), lambda b, i: (b, 0, 0), pipeline_mode=one),
            pl.BlockSpec((META_TILE, LANES), const(2)),
            pl.BlockSpec((META_TILE, ATTN_WIDTH), const(2)),
            pl.BlockSpec((ATTN_WIDTH, META_TILE), const(2)),
            pl.BlockSpec((2, N_HEADS, TQ, TQ), const(4), pipeline_mode=one),
            pl.BlockSpec((N_HEADS, META_TILE, TQ), const(3), pipeline_mode=one),
        ],
        out_specs=pl.BlockSpec((1, TQ, ATTN_WIDTH), lambda b, i: (b, i, 0)),
        out_shape=jax.ShapeDtypeStruct((bsz, seq, ATTN_WIDTH), f32),
        scratch_shapes=[
            pltpu.VMEM((n_tiles, WORD_BITS, SUBLANES, TQ), i32),
            pltpu.VMEM((META_KG, SUBLANES, TQ), i32),
            pltpu.VMEM((n_tiles, SUBLANES, TQ), i32),
            pltpu.VMEM((n_tiles, SUBLANES, TQ), i32),
            pltpu.VMEM((META_KG, SUBLANES, TQ), i32),
            pltpu.VMEM((META_KG, SUBLANES, TQ), i32),
            pltpu.VMEM((SUBLANES, TQ), f32),
            pltpu.VMEM((2, TQ, TQ), i32),
            pltpu.VMEM((N_HEADS, PAIR, TQ), bf16),
            pltpu.VMEM((TQ, TQ), f32),
            pltpu.VMEM((ATTN_WIDTH, TQ), f32),
        ] + per_head((SUBLANES, TQ), f32)
          + per_head((SUBLANES, TQ), f32)
          + per_head((HEAD_DIM, TQ), f32)
          + per_head((TQ, TQ), f32)
          + per_head((TQ, TQ), bf16)
          + per_head((SUBLANES, TQ), f32),
        compiler_params=_cparams(2),
    )(qit, wit, qt, kib, kb, vtb, kim, km, vmt, bias_pp, bias_meta)


PAGES_PER_STEP = 32
IDX_PAGES_PER_STEP = 64
SROWS = 8
SAMPLE_SUB = 16


def _sample_scores_body(pt_ref, qi_ref, wb_ref, kin_ref, *rest):
    page_refs = rest[:IDX_PAGES_PER_STEP]
    out_ref, outn_ref = rest[IDX_PAGES_PER_STEP:]
    qi = qi_ref[0]
    wb = wb_ref[0]

    def combine(s):
        t = jnp.maximum(s, 0.0) * wb
        acc = t[0:SROWS]
        for h in range(1, N_IDX_HEADS):
            acc = acc + t[SROWS * h:SROWS * (h + 1)]
        return acc

    zpad = jnp.zeros((LANES - IDX_DIM, LANES), bf16)
    for jx in range(IDX_PAGES_PER_STEP):
        kt = jnp.concatenate([page_refs[jx][0].astype(bf16), zpad], axis=0)
        out_ref[0, :, LANES * jx:LANES * (jx + 1)] = combine(
            jnp.dot(qi, kt, preferred_element_type=f32))
    outn_ref[0] = combine(lax.dot_general(qi, kin_ref[0], NT_DIMS, preferred_element_type=f32))


def _sample_scores(page_table, qi32, wb32, kin2, kidx_t):
    dbsz, n_pages = page_table.shape
    page = kidx_t.shape[2]
    assert page == LANES and n_pages % IDX_PAGES_PER_STEP == 0
    nsteps = n_pages // IDX_PAGES_PER_STEP
    mrows = N_IDX_HEADS * SROWS

    def page_spec(jx):
        return pl.BlockSpec((1, IDX_DIM, page),
                            lambda b, s, pt: (pt[b, s * IDX_PAGES_PER_STEP + jx], 0, 0))

    grid_spec = pltpu.PrefetchScalarGridSpec(
        num_scalar_prefetch=1,
        grid=(dbsz, nsteps),
        in_specs=[pl.BlockSpec((1, mrows, LANES), lambda b, s, pt: (b, 0, 0)),
                  pl.BlockSpec((1, mrows, LANES), lambda b, s, pt: (b, 0, 0)),
                  pl.BlockSpec((1, LANES, LANES), lambda b, s, pt: (b, 0, 0))]
                 + [page_spec(jx) for jx in range(IDX_PAGES_PER_STEP)],
        out_specs=[pl.BlockSpec((1, SROWS, IDX_PAGES_PER_STEP * LANES), lambda b, s, pt: (b, 0, s)),
                   pl.BlockSpec((1, SROWS, LANES), lambda b, s, pt: (b, 0, 0))],
    )
    return pl.pallas_call(
        _sample_scores_body,
        grid_spec=grid_spec,
        out_shape=[jax.ShapeDtypeStruct((dbsz, SROWS, n_pages * page), f32),
                   jax.ShapeDtypeStruct((dbsz, SROWS, LANES), f32)],
        compiler_params=_cparams(2),
    )(page_table, qi32, wb32, kin2, *([kidx_t] * IDX_PAGES_PER_STEP))


SEL_ROWS = 128


def _sample_select_body(sp_ref, sn_ref, mp_ref, mn_ref, planes_scr, alive_scr, sel_scr, krem_scr,
                        *, n_groups, n_new, topk):
    n_chunks = sp_ref.shape[1] // LANES

    def key_chunk(c, carry):
        col = pl.ds(pl.multiple_of(c * LANES, LANES), LANES)
        planes_scr[c // WORD_BITS, c % WORD_BITS] = _score_key(sp_ref[:, col])
        return carry

    lax.fori_loop(0, n_chunks, key_chunk, 0)
    for c in range(n_chunks, n_groups * WORD_BITS):
        planes_scr[c // WORD_BITS, c % WORD_BITS] = jnp.full((SEL_ROWS, LANES), INT_MIN, i32)
    lane = lax.broadcasted_iota(i32, (SEL_ROWS, LANES), 1)
    row = lax.broadcasted_iota(i32, (SEL_ROWS, LANES), 0)
    causal_new = (lane <= (row & (SROWS - 1))) & (lane < n_new)
    key_n = jnp.where(causal_new, _score_key(sn_ref[...]), INT_MIN) ^ jnp.int32(INT_MIN)

    for g in range(n_groups):
        _transpose_group(planes_scr, g, SEL_ROWS)
    full = jnp.full((SEL_ROWS, LANES), -1, i32)
    alive_init = [jnp.full((SEL_ROWS, LANES), 1, i32)] + [full] * n_groups
    k_row = jnp.full((SEL_ROWS, LANES), topk, f32)
    _radix_select(planes_scr, key_n, alive_scr, sel_scr, krem_scr, n_groups, False, alive_init, k_row)

    def mask_chunk(c, carry):
        col = pl.ds(pl.multiple_of(c * LANES, LANES), LANES)
        mp_ref[:, col] = _mask_from_bits(sel_scr[1 + c // WORD_BITS], c % WORD_BITS)
        return carry

    lax.fori_loop(0, n_chunks, mask_chunk, 0)
    mn_ref[...] = _mask_from_bits(sel_scr[0], 0)


def _sample_select(scores_past, scores_new, n_new, topk):
    rows, past = scores_past.shape
    assert rows % SEL_ROWS == 0 and past % LANES == 0
    n_groups = -(-past // GROUP_KEYS)
    body = functools.partial(_sample_select_body, n_groups=n_groups, n_new=n_new, topk=topk)
    return pl.pallas_call(
        body,
        grid=(rows // SEL_ROWS,),
        in_specs=[pl.BlockSpec((SEL_ROWS, past), lambda r: (r, 0)),
                  pl.BlockSpec((SEL_ROWS, LANES), lambda r: (r, 0))],
        out_specs=[pl.BlockSpec((SEL_ROWS, past), lambda r: (r, 0)),
                   pl.BlockSpec((SEL_ROWS, LANES), lambda r: (r, 0))],
        out_shape=[jax.ShapeDtypeStruct((rows, past), f32),
                   jax.ShapeDtypeStruct((rows, LANES), f32)],
        scratch_shapes=[
            pltpu.VMEM((n_groups, WORD_BITS, SEL_ROWS, LANES), i32),
            pltpu.VMEM((n_groups + 1, SEL_ROWS, LANES), i32),
            pltpu.VMEM((n_groups + 1, SEL_ROWS, LANES), i32),
            pltpu.VMEM((SEL_ROWS, LANES), f32),
        ],
        compiler_params=_cparams(1),
    )(scores_past, scores_new)


def _sample_attend_body(pt_ref, q_ref, mp_ref, mn_ref, knt_ref, vnt_ref, bs_ref, bn_ref, *rest):
    k_refs = rest[:PAGES_PER_STEP]
    v_refs = rest[PAGES_PER_STEP:2 * PAGES_PER_STEP]
    o_ref, m_scr, l_scr, acc_scr, kcat_scr, vcat_scr, s_scr, p_scr, alpha_scr = rest[2 * PAGES_PER_STEP:]
    s_idx = pl.program_id(1)
    last_step = s_idx == pl.num_programs(1) - 1
    mrows = N_HEADS * SROWS
    q = q_ref[0]

    @pl.when(s_idx == 0)
    def _():
        m_scr[...] = jnp.full((mrows, LANES), -jnp.inf, f32)
        l_scr[...] = jnp.zeros((mrows, LANES), f32)
        acc_scr[...] = jnp.zeros((mrows, ATTN_WIDTH), f32)

    def attend(width, bias_tail, maskadd):
        s_scr[:, 0:width] = jnp.dot(q, kcat_scr[:, 0:width], preferred_element_type=f32)
        mk = jnp.concatenate([maskadd] * (SAMPLE_SUB // SROWS), axis=0)
        for r0 in range(0, mrows, SAMPLE_SUB):
            rows = slice(r0, r0 + SAMPLE_SUB)
            s = s_scr[rows, 0:width] + mk
            tail = s[:, width - LANES:] + bias_tail[rows]
            s = tail if width == LANES else jnp.concatenate([s[:, :width - LANES], tail], axis=1)
            m_old = m_scr[rows]
            m_new = jnp.maximum(m_old, jnp.max(s, axis=-1, keepdims=True))
            alpha = jnp.exp2(m_old - m_new)
            m_scr[rows] = m_new
            m_rep = m_new if width == LANES else jnp.concatenate([m_new] * (width // LANES), axis=1)
            p = jnp.exp2(s - m_rep)
            l_scr[rows] = alpha * l_scr[rows] + jnp.sum(p, axis=-1, keepdims=True)
            p_scr[rows, 0:width] = p.astype(bf16)
            alpha_scr[rows] = alpha
        o = lax.dot_general(p_scr[:, 0:width], vcat_scr[:, 0:width], NT_DIMS, preferred_element_type=f32)
        aw = jnp.concatenate([alpha_scr[...]] * (ATTN_WIDTH // LANES), axis=1)
        acc_scr[...] = aw * acc_scr[...] + o

    for jx in range(PAGES_PER_STEP):
        kcat_scr[:, LANES * jx:LANES * (jx + 1)] = k_refs[jx][0].astype(bf16)
        vcat_scr[:, LANES * jx:LANES * (jx + 1)] = v_refs[jx][0].astype(bf16)
    attend(PAGES_PER_STEP * LANES, bs_ref[...] * jnp.where(last_step, 1.0, 0.0), mp_ref[0])

    @pl.when(last_step)
    def _():
        kcat_scr[:, 0:LANES] = knt_ref[0]
        vcat_scr[:, 0:LANES] = vnt_ref[0]
        attend(LANES, bn_ref[...], mn_ref[0])
        l_w = jnp.concatenate([l_scr[...]] * (ATTN_WIDTH // LANES), axis=1)
        accn = acc_scr[...] / l_w
        head_of_lane = lax.broadcasted_iota(i32, (SROWS, ATTN_WIDTH), 1) // HEAD_DIM
        out = jnp.zeros((SROWS, ATTN_WIDTH), f32)
        for h in range(N_HEADS):
            out = out + jnp.where(head_of_lane == h, accn[SROWS * h:SROWS * (h + 1)], 0.0)
        o_ref[0] = out


def _sample_attend(page_table, q64, mask_past, mask_new, knt, vnt, bias_s, bias_n, k_t, v_t):
    dbsz, n_pages = page_table.shape
    page = k_t.shape[2]
    assert page == LANES and n_pages % PAGES_PER_STEP == 0
    nsteps = n_pages // PAGES_PER_STEP
    mrows = N_HEADS * SROWS
    width = PAGES_PER_STEP * LANES

    def page_spec(jx):
        return pl.BlockSpec((1, ATTN_WIDTH, page), lambda b, s, pt: (pt[b, s * PAGES_PER_STEP + jx], 0, 0))

    grid_spec = pltpu.PrefetchScalarGridSpec(
        num_scalar_prefetch=1,
        grid=(dbsz, nsteps),
        in_specs=[pl.BlockSpec((1, mrows, ATTN_WIDTH), lambda b, s, pt: (b, 0, 0)),
                  pl.BlockSpec((1, SROWS, width), lambda b, s, pt: (b, 0, s)),
                  pl.BlockSpec((1, SROWS, LANES), lambda b, s, pt: (b, 0, 0)),
                  pl.BlockSpec((1, ATTN_WIDTH, LANES), lambda b, s, pt: (b, 0, 0)),
                  pl.BlockSpec((1, ATTN_WIDTH, LANES), lambda b, s, pt: (b, 0, 0)),
                  pl.BlockSpec((mrows, LANES), lambda b, s, pt: (0, 0)),
                  pl.BlockSpec((mrows, LANES), lambda b, s, pt: (0, 0))]
                 + [page_spec(jx) for jx in range(PAGES_PER_STEP)]
                 + [page_spec(jx) for jx in range(PAGES_PER_STEP)],
        out_specs=pl.BlockSpec((1, SROWS, ATTN_WIDTH), lambda b, s, pt: (b, 0, 0)),
        scratch_shapes=[pltpu.VMEM((mrows, LANES), f32),
                        pltpu.VMEM((mrows, LANES), f32),
                        pltpu.VMEM((mrows, ATTN_WIDTH), f32),
                        pltpu.VMEM((ATTN_WIDTH, width), bf16),
                        pltpu.VMEM((ATTN_WIDTH, width), bf16),
                        pltpu.VMEM((mrows, width), f32),
                        pltpu.VMEM((mrows, width), bf16),
                        pltpu.VMEM((mrows, LANES), f32)],
    )
    return pl.pallas_call(
        _sample_attend_body,
        grid_spec=grid_spec,
        out_shape=jax.ShapeDtypeStruct((dbsz, SROWS, ATTN_WIDTH), f32),
        compiler_params=_cparams(2),
    )(page_table, q64, mask_past, mask_new, knt, vnt, bias_s, bias_n,
      *([k_t] * PAGES_PER_STEP), *([v_t] * PAGES_PER_STEP))


def _silu(x):
    return x * jax.nn.sigmoid(x)


def _merge_core(x, d, gp, ao, ga, gtp, gta, pw_ref, ps_ref, wbp_ref, wba_ref, wo_ref, lng_ref, lnb_ref):
    gw = POOL_WIDTH // len(POOL_WINDOWS)
    parts = [jnp.dot(d[:, gw * g:gw * (g + 1)].astype(bf16), pw_ref[g], preferred_element_type=f32)
             for g in range(len(POOL_WINDOWS))]
    pool_o = jnp.concatenate(parts, axis=1) * ps_ref[...]
    bp = jnp.dot((pool_o * _silu(gp)).astype(bf16), wbp_ref[...], preferred_element_type=f32)
    ba = jnp.dot((ao * _silu(ga)).astype(bf16), wba_ref[...], preferred_element_type=f32)
    m = jax.nn.sigmoid(gtp) * bp + jax.nn.sigmoid(gta) * ba
    out = jnp.dot(m.astype(bf16), wo_ref[...], preferred_element_type=f32)
    z = ALPHA * x + out
    mu = jnp.mean(z, axis=-1, keepdims=True)
    zc = z - mu
    var = jnp.mean(zc * zc, axis=-1, keepdims=True)
    return zc * lax.rsqrt(var + LN_EPS) * lng_ref[...] + lnb_ref[...]


def _merge_prompt_body(x_ref, u_ref, uh_ref, um_ref, gp_ref, ao_ref, ga_ref, gtp_ref, gta_ref,
                       pw_ref, ps_ref, wbp_ref, wba_ref, wo_ref, lng_ref, lnb_ref, y_ref, uu_scr):
    t = pl.program_id(1)
    halo = N_META
    uu_scr[0:halo, :] = jnp.where(t == 0, um_ref[...], uh_ref[0])
    uu_scr[halo:halo + TQ, :] = u_ref[0]
    gw = POOL_WIDTH // len(POOL_WINDOWS)
    parts = []
    for g, w in enumerate(POOL_WINDOWS):
        cols = slice(gw * g, gw * (g + 1))
        s = uu_scr[halo:halo + TQ, cols]
        for j in range(1, w):
            s = s + uu_scr[halo - j:halo - j + TQ, cols]
        parts.append(s * (1.0 / w) - uu_scr[halo:halo + TQ, cols])
    d = jnp.concatenate(parts, axis=1)
    y_ref[0] = _merge_core(x_ref[0], d, gp_ref[0], ao_ref[0], ga_ref[0], gtp_ref[0], gta_ref[0],
                           pw_ref, ps_ref, wbp_ref, wba_ref, wo_ref, lng_ref, lnb_ref)


def _weight_specs(nd_grid):
    z = lambda nd: (lambda *a: (0,) * nd)
    gw = POOL_WIDTH // len(POOL_WINDOWS)
    return [pl.BlockSpec((len(POOL_WINDOWS), gw, gw), z(3)),
            pl.BlockSpec((1, POOL_WIDTH), z(2)),
            pl.BlockSpec((POOL_WIDTH, D_MODEL), z(2)),
            pl.BlockSpec((ATTN_WIDTH, D_MODEL), z(2)),
            pl.BlockSpec((D_MODEL, D_MODEL), z(2)),
            pl.BlockSpec((1, D_MODEL), z(2)),
            pl.BlockSpec((1, D_MODEL), z(2))]


def _merge_prompt(x, u, u_meta, gp, ao, ga, gtp, gta, weights):
    bsz, seq, _ = x.shape
    nt = seq // TQ
    hb = TQ // N_META
    tile = lambda n: pl.BlockSpec((1, TQ, n), lambda b, t: (b, t, 0))
    return pl.pallas_call(
        _merge_prompt_body,
        grid=(bsz, nt),
        in_specs=[tile(D_MODEL), tile(POOL_WIDTH),
                  pl.BlockSpec((1, N_META, POOL_WIDTH), lambda b, t: (b, jnp.maximum(t * hb - 1, 0), 0)),
                  pl.BlockSpec((N_META, POOL_WIDTH), lambda b, t: (0, 0)),
                  tile(POOL_WIDTH), tile(ATTN_WIDTH), tile(ATTN_WIDTH), tile(D_MODEL), tile(D_MODEL)]
                 + _weight_specs(2),
        out_specs=tile(D_MODEL),
        out_shape=jax.ShapeDtypeStruct((bsz, seq, D_MODEL), f32),
        scratch_shapes=[pltpu.VMEM((N_META + TQ, POOL_WIDTH), f32)],
        compiler_params=_cparams(2),
    )(x, u, u, u_meta, gp, ao, ga, gtp, gta, *weights)


def _merge_sample_body(x_ref, u_ref, sp_ref, gp_ref, ao_ref, ga_ref, gtp_ref, gta_ref,
                       pw_ref, ps_ref, wbp_ref, wba_ref, wo_ref, lng_ref, lnb_ref, y_ref, *, tnew):
    gw = POOL_WIDTH // len(POOL_WINDOWS)

    def hist(t, cols):
        if t < POOL_BUF:
            return sp_ref[:, POOL_WIDTH * t + cols.start:POOL_WIDTH * t + cols.stop]
        tt = t - POOL_BUF
        return u_ref[:, POOL_WIDTH * tt + cols.start:POOL_WIDTH * tt + cols.stop]

    for i in range(tnew):
        parts = []
        for g, w in enumerate(POOL_WINDOWS):
            cols = slice(gw * g, gw * (g + 1))
            s = hist(POOL_BUF + i, cols)
            for j in range(1, w):
                s = s + hist(POOL_BUF + i - j, cols)
            parts.append(s * (1.0 / w) - hist(POOL_BUF + i, cols))
        d = jnp.concatenate(parts, axis=1)
        c5 = slice(POOL_WIDTH * i, POOL_WIDTH * (i + 1))
        c10 = slice(D_MODEL * i, D_MODEL * (i + 1))
        y_ref[:, c10] = _merge_core(x_ref[:, c10], d, gp_ref[:, c5], ao_ref[:, c5], ga_ref[:, c5],
                                    gtp_ref[:, c10], gta_ref[:, c10],
                                    pw_ref, ps_ref, wbp_ref, wba_ref, wo_ref, lng_ref, lnb_ref)


def _merge_sample(x, u, state_pool, gp, ao, ga, gtp, gta, weights, tnew):
    dbsz = x.shape[0]
    full = lambda a: pl.BlockSpec(a.shape, lambda i: (0, 0))
    args = (x, u, state_pool, gp, ao, ga, gtp, gta)
    body = functools.partial(_merge_sample_body, tnew=tnew)
    return pl.pallas_call(
        body,
        grid=(1,),
        in_specs=[full(a) for a in args] + _weight_specs(1),
        out_specs=pl.BlockSpec((dbsz, tnew * D_MODEL), lambda i: (0, 0)),
        out_shape=jax.ShapeDtypeStruct((dbsz, tnew * D_MODEL), f32),
        compiler_params=_cparams(1),
    )(*args, *weights)


def _rel_bucket(n):
    max_exact = N_BUCKETS // 2
    nf = jnp.maximum(n, 1).astype(f32)
    large = max_exact + (jnp.log(nf / max_exact) / math.log(MAX_DISTANCE / max_exact)
                         * (N_BUCKETS - max_exact)).astype(i32)
    large = jnp.minimum(large, N_BUCKETS - 1)
    return jnp.where(n < max_exact, n, large)


def _bias_table(rel_bias, dist):
    bucket = _rel_bucket(jnp.maximum(dist, 0))
    rb = rel_bias.astype(f32)
    far = rb[N_BUCKETS - 1]
    shape = (N_HEADS,) + (1,) * dist.ndim
    out = jnp.zeros((N_HEADS,) + dist.shape, f32)
    for b in range(N_BUCKETS):
        out = jnp.where(bucket[None] == b, (rb[b] - far).reshape(shape), out)
    return out * LOG2E


def _split_w_in(w_in):
    splits = (512, 512, 512, 512, 512, 512, 256, 64, 4, 1024, 1024)
    offs = np.cumsum((0,) + splits)
    return {n: w_in[:, offs[i]:offs[i + 1]] for i, n in enumerate(
        ("u", "gp", "q", "k", "v", "ga", "qi", "ki", "wi", "gate_p", "gate_a"))}


def _pad_w_in(w_in):
    part = _split_w_in(w_in)
    cols = [part["u"], part["gp"], part["k"], part["ga"], part["ki"], part["ki"],
            part["gate_p"], part["gate_a"]]
    w = jnp.concatenate(cols, axis=1)
    assert w.shape[1] == W_COLS
    zrows = jnp.zeros((w_in.shape[0], BF16_ROWS - N_IDX_HEADS), w_in.dtype)
    wt = jnp.concatenate([part["k"], part["v"], part["q"], part["qi"] * (IDX_DIM ** -0.5), part["ki"],
                          part["wi"] * (N_IDX_HEADS ** -0.5), zrows], axis=1).T
    assert wt.shape[0] == WT_ROWS
    return w.astype(bf16), wt.astype(bf16)


def kernel(x_prompt, x_sample, cache_k, cache_v, cache_kidx, state_pool, page_table, meta, w_in,
           pool_w, pool_scale, w_br_pool, w_br_attn, rel_bias, w_out, ln_g, ln_b):
    bsz, seq, _ = x_prompt.shape
    dbsz, tnew, _ = x_sample.shape
    n_phys, page = cache_k.shape[0], cache_k.shape[1]
    n_pages = page_table.shape[1]
    past_len = n_pages * page
    assert (IDX_DIM ** -0.5, N_IDX_HEADS ** -0.5) == (0.125, 0.5)
    assert tnew <= SROWS and N_META <= LANES and tnew <= POOL_BUF

    w_pad, wt_pad = _pad_w_in(w_in)
    weights = (pool_w.astype(bf16), pool_scale.reshape(1, POOL_WIDTH).astype(f32),
               w_br_pool.astype(bf16), w_br_attn.astype(bf16), w_out.astype(bf16),
               ln_g.reshape(1, D_MODEL).astype(f32), ln_b.reshape(1, D_MODEL).astype(f32))

    (u, gp, kb, ga, kib, gtp, gta, kt, vt, vtb, qt, qit, kidxt, wit) = _project(
        x_prompt.reshape(bsz * seq, D_MODEL), w_pad, wt_pad, TQ, groups=bsz)
    (u_m, _, kb_m, _, kib_m, _, _, kt_m, vt_m, vtb_m, _, _, kidxt_m, _) = _project(
        meta.astype(x_prompt.dtype), w_pad, wt_pad, N_META)
    r3 = lambda a: a.reshape(bsz, seq, a.shape[-1])
    pad_rows = lambda a: jnp.pad(a, ((0, META_TILE - N_META), (0, 0)))
    pad_cols = lambda a: jnp.pad(a, ((0, 0), (0, META_TILE - N_META)))

    ar = jnp.arange(TQ)
    assert TQ >= MAX_DISTANCE
    d_pp = jnp.stack([ar[None, :] - ar[:, None] + TQ * dd for dd in range(2)])
    bias_pp = jnp.swapaxes(_bias_table(rel_bias, d_pp), 0, 1)
    d_m = ar[None, :] + N_META - jnp.arange(META_TILE)[:, None]
    bias_meta = _bias_table(rel_bias, d_m)

    topk_p = min(TOPK_MAX, seq // 4)
    attn_o = _attn_prompt(qit, wit, qt, r3(kib), r3(kb), vtb,
                          pad_rows(kib_m), pad_rows(kb_m), pad_cols(vtb_m[0]), bias_pp, bias_meta, topk_p)
    y_prompt = _merge_prompt(x_prompt, r3(u), u_m, r3(gp), attn_o, r3(ga), r3(gtp), r3(gta), weights)

    def with_meta_t(m_t, a_t):
        m3 = jnp.broadcast_to(m_t, (bsz,) + m_t.shape[1:])
        return jnp.swapaxes(jnp.concatenate([m3, a_t], axis=2), 1, 2)

    k_prompt = with_meta_t(kt_m, kt).reshape(bsz, seq + N_META, N_HEADS, HEAD_DIM)
    v_prompt = with_meta_t(vt_m, vt).reshape(bsz, seq + N_META, N_HEADS, HEAD_DIM)
    kidx_prompt = with_meta_t(kidxt_m, kidxt)
    pool_prompt = r3(u)[:, seq - POOL_BUF:]

    (us, gps, _, gas, kibs, gtps, gtas, kts, vts, _, qts, qits, kidxts, wits) = _project(
        x_sample.reshape(dbsz * tnew, D_MODEL), w_pad, wt_pad, min(TQ, dbsz * tnew))
    s3 = lambda a: a.reshape(dbsz, tnew, a.shape[-1])
    t3 = lambda a_t: jnp.transpose(a_t.reshape(a_t.shape[1], dbsz, tnew), (1, 2, 0))
    padq = lambda a: jnp.pad(a, ((0, 0), (0, 0), (0, SROWS - tnew), (0, 0)))
    qi4 = jnp.swapaxes(t3(qits).reshape(dbsz, tnew, N_IDX_HEADS, IDX_DIM), 1, 2)
    qi32 = jnp.pad(padq(qi4), ((0, 0), (0, 0), (0, 0), (0, LANES - IDX_DIM)))
    qi32 = qi32.reshape(dbsz, N_IDX_HEADS * SROWS, LANES)
    wi4 = jnp.swapaxes(t3(wits)[:, :, :N_IDX_HEADS], 1, 2)[..., None]
    wb32 = jnp.broadcast_to(padq(wi4), (dbsz, N_IDX_HEADS, SROWS, LANES))
    wb32 = wb32.reshape(dbsz, N_IDX_HEADS * SROWS, LANES)
    kin2 = jnp.pad(s3(kibs), ((0, 0), (0, LANES - tnew), (0, 0)))
    kidx_t = jnp.swapaxes(cache_kidx, 1, 2)
    k_t = jnp.transpose(cache_k, (0, 2, 3, 1)).reshape(n_phys, ATTN_WIDTH, page)
    v_t = jnp.transpose(cache_v, (0, 2, 3, 1)).reshape(n_phys, ATTN_WIDTH, page)
    scores_past, scores_new = _sample_scores(page_table, qi32, wb32, kin2, kidx_t)

    topk_s = min(TOPK_MAX, (past_len + tnew) // 4)
    mask_past, mask_new = _sample_select(scores_past.reshape(dbsz * SROWS, past_len),
                                         scores_new.reshape(dbsz * SROWS, LANES), tnew, topk_s)

    head_of_lane = jnp.arange(ATTN_WIDTH) // HEAD_DIM
    q4 = jnp.where(head_of_lane[None, None, None, :] == jnp.arange(N_HEADS)[None, :, None, None],
                   t3(qts)[:, None, :, :], jnp.zeros((), bf16))
    q64 = padq(q4).reshape(dbsz, N_HEADS * SROWS, ATTN_WIDTH)
    new_t = lambda a_t: jnp.pad(jnp.swapaxes(a_t.reshape(ATTN_WIDTH, dbsz, tnew), 0, 1).astype(bf16),
                                ((0, 0), (0, 0), (0, LANES - tnew)))
    qrow = jnp.arange(SROWS)
    assert page >= MAX_DISTANCE
    d_s = (page + qrow)[:, None] - jnp.arange(LANES)[None, :]
    bias_s = _bias_table(rel_bias, d_s).reshape(N_HEADS * SROWS, LANES)
    d_n = qrow[:, None] - jnp.arange(LANES)[None, :]
    bias_n = _bias_table(rel_bias, d_n).reshape(N_HEADS * SROWS, LANES)
    ao_s = _sample_attend(page_table, q64,
                          mask_past.reshape(dbsz, SROWS, past_len), mask_new.reshape(dbsz, SROWS, LANES),
                          new_t(kts), new_t(vts), bias_s, bias_n, k_t, v_t)
    ao_s = ao_s[:, :tnew].reshape(dbsz, tnew * ATTN_WIDTH)

    flat = lambda a: a.reshape(dbsz, -1)
    y_sample = _merge_sample(flat(x_sample), flat(us), flat(state_pool.astype(f32)), flat(gps), ao_s,
                             flat(gas), flat(gtps), flat(gtas), weights, tnew)
    y_sample = y_sample.reshape(dbsz, tnew, D_MODEL)

    k_sample = t3(kts).reshape(dbsz, tnew, N_HEADS, HEAD_DIM)
    v_sample = t3(vts).reshape(dbsz, tnew, N_HEADS, HEAD_DIM)
    kidx_sample = t3(kidxts)
    pool_sample = jnp.concatenate([state_pool.astype(f32), s3(us)], axis=1)[:, -POOL_BUF:]

    return (y_prompt, y_sample, k_prompt, v_prompt, kidx_prompt, pool_prompt,
            k_sample, v_sample, kidx_sample, pool_sample)
```

```python
import functools
import math

import numpy as np
import jax
import jax.numpy as jnp
from jax import lax
from jax.experimental import pallas as pl
from jax.experimental.pallas import tpu as pltpu

f32 = jnp.float32
bf16 = jnp.bfloat16
i32 = jnp.int32

N_META = 16
POOL_WINDOWS = (2, 4, 8, 16)
POOL_BUF = max(POOL_WINDOWS) - 1
N_HEADS = 8
HEAD_DIM = 64
N_IDX_HEADS = 4
IDX_DIM = 64
TOPK_MAX = 256
N_BUCKETS = 32
MAX_DISTANCE = 128
DEPTH = 1
ALPHA = (2 * DEPTH) ** 0.25
LN_EPS = 1e-5
MASKED_LOGIT = -1e30

LANES = 128
SUBLANES = 8
BF16_ROWS = 16
WORD_BITS = 32
VMEM_LIMIT = 56 * 1024 * 1024

TQ = 256
GROUP_KEYS = WORD_BITS * LANES
INT_MIN = -(2 ** 31)
LOG2E = math.log2(math.e)
Q_SCALE = LOG2E * HEAD_DIM ** -0.5

D_MODEL = 1024
POOL_WIDTH = 512
ATTN_WIDTH = N_HEADS * HEAD_DIM
PAIR = 2 * HEAD_DIM

SEG = {
    "u": (0, 512), "gp": (512, 512), "k": (1024, 512), "ga": (1536, 512), "ki2": (2048, 128),
    "gate_p": (2176, 1024), "gate_a": (3200, 1024),
}
W_COLS = 4224
TSEG = {"kT": (0, 512), "vT": (512, 512), "qT": (1024, 512), "qiT": (1536, 256),
        "kidxT": (1792, IDX_DIM), "wiT": (1792 + IDX_DIM, BF16_ROWS)}
WT_ROWS = 1792 + IDX_DIM + BF16_ROWS
NT_DIMS = (((1,), (1,)), ((), ()))


def _cparams(n_axes):
    return pltpu.CompilerParams(dimension_semantics=("arbitrary",) * n_axes,
                                vmem_limit_bytes=VMEM_LIMIT)


def _project_body(x_ref, w_ref, wt_ref, u_o, gp_o, kb_o, ga_o, kib_o, gtp_o, gta_o,
                  kt_o, vt_o, vtb_o, qt_o, qit_o, kidxt_o, wit_o):
    xb = x_ref[...].astype(bf16)

    def seg(name):
        off, n = SEG[name]
        return jnp.dot(xb, w_ref[:, off:off + n], preferred_element_type=f32)

    def tseg(name):
        off, n = TSEG[name]
        return lax.dot_general(wt_ref[off:off + n, :], xb, NT_DIMS, preferred_element_type=f32)

    u_o[...] = seg("u")
    gp_o[...] = seg("gp")
    kb_o[...] = seg("k").astype(bf16)
    ga_o[...] = seg("ga")
    kib_o[...] = seg("ki2").astype(bf16)
    gtp_o[...] = seg("gate_p")
    gta_o[...] = seg("gate_a")
    kt_o[0] = tseg("kT")
    vt = tseg("vT")
    vt_o[0] = vt
    vtb_o[0] = vt.astype(bf16)
    qt_o[0] = (tseg("qT") * Q_SCALE).astype(bf16)
    qit_o[0] = tseg("qiT").astype(bf16)
    kidxt_o[0] = tseg("kidxT")
    wit_o[0] = tseg("wiT")[0:SUBLANES]


def _project(x2d, w_pad, wt_pad, tm, groups=1):
    rows = x2d.shape[0]
    assert rows % (groups * tm) == 0
    per_group = rows // (groups * tm)
    row_outs = [(512, f32), (512, f32), (512, bf16), (512, f32), (128, bf16), (1024, f32), (1024, f32)]
    col_outs = [(512, f32), (512, f32), (512, bf16), (512, bf16), (256, bf16), (IDX_DIM, f32),
                (SUBLANES, f32)]
    out_shape = ([jax.ShapeDtypeStruct((rows, n), dt) for n, dt in row_outs]
                 + [jax.ShapeDtypeStruct((groups, n, rows // groups), dt) for n, dt in col_outs])
    out_specs = ([pl.BlockSpec((tm, n), lambda i: (i, 0)) for n, _ in row_outs]
                 + [pl.BlockSpec((1, n, tm), lambda i: (i // per_group, 0, i % per_group))
                    for n, _ in col_outs])
    return pl.pallas_call(
        _project_body,
        grid=(rows // tm,),
        in_specs=[pl.BlockSpec((tm, D_MODEL), lambda i: (i, 0)),
                  pl.BlockSpec((D_MODEL, W_COLS), lambda i: (0, 0)),
                  pl.BlockSpec((WT_ROWS, D_MODEL), lambda i: (0, 0))],
        out_specs=out_specs,
        out_shape=out_shape,
        compiler_params=_cparams(1),
    )(x2d, w_pad, wt_pad)


def _score_key(score):
    bits = lax.bitcast_convert_type(score, i32)
    sign = lax.shift_right_arithmetic(bits, jnp.full(bits.shape, 31, i32))
    mag = bits & jnp.int32(0x7FFFFFFF)
    return (mag ^ sign) - sign


def _bit_transpose32(words):
    a = list(words)
    j = 16
    m = 0x0000FFFF
    while j != 0:
        k = 0
        sh = jnp.full(a[0].shape, j, i32)
        mm = jnp.int32(np.uint32(m).astype(np.int32))
        while k < 32:
            t = (a[k] ^ lax.shift_right_logical(a[k + j], sh)) & mm
            a[k] = a[k] ^ t
            a[k + j] = a[k + j] ^ lax.shift_left(t, sh)
            k = (k + j + 1) & ~j
        j >>= 1
        if j:
            m = (m ^ (m << j)) & 0xFFFFFFFF
    return a


def _bit_planes(words):
    t = _bit_transpose32(words[::-1])
    planes = t[::-1]
    planes[WORD_BITS - 1] = ~planes[WORD_BITS - 1]
    return planes


def _transpose_group(planes_scr, g, rows_total):
    def body(rg, carry):
        rows = pl.ds(pl.multiple_of(rg * SUBLANES, SUBLANES), SUBLANES)
        planes = _bit_planes([planes_scr[g, c, rows, :] for c in range(WORD_BITS)])
        for b in range(WORD_BITS):
            planes_scr[g, b, rows, :] = planes[b]
        return carry

    lax.fori_loop(0, rows_total // SUBLANES, body, 0)


def _lane_total(cnt, ones_mat):
    return jnp.dot(cnt.astype(f32).astype(bf16), ones_mat, preferred_element_type=f32)


def _radix_select(planes_scr, single_key, alive_scr, sel_scr, krem_scr, n_groups, single_first,
                  alive_init, k_row):
    rows = single_key.shape[0]
    nset = n_groups + 1
    ones_mat = jnp.ones((LANES, LANES), bf16)
    zero = jnp.zeros((rows, LANES), i32)

    for s in range(nset):
        alive_scr[s] = alive_init[s]
        sel_scr[s] = zero
    krem_scr[...] = k_row

    def step(planes):
        alive = [alive_scr[s] for s in range(nset)]
        ones = [alive[s] & planes[s] for s in range(nset)]
        cnt = ones[0]
        for s in range(1, nset):
            cnt = cnt + lax.population_count(ones[s])
        tot = _lane_total(cnt, ones_mat)
        krem = krem_scr[...]
        take = tot >= krem
        for s in range(nset):
            alive_scr[s] = jnp.where(take, ones[s], alive[s] ^ ones[s])
            sel_scr[s] = sel_scr[s] | jnp.where(take, zero, ones[s])
        krem_scr[...] = jnp.where(take, krem, krem - tot)

    def value_step(t, carry):
        b = WORD_BITS - 1 - t
        sh = jnp.full((rows, LANES), b, i32)
        planes = [lax.shift_right_logical(single_key, sh) & 1]
        for g in range(n_groups):
            planes.append(planes_scr[g, b])
        step(planes)
        return carry

    lax.fori_loop(0, WORD_BITS, value_step, 0)

    full = jnp.full((rows, LANES), -1, i32)
    one = jnp.full((rows, LANES), 1, i32)
    codes = [0] + [g + 1 for g in range(n_groups)] if single_first else [n_groups] + list(range(n_groups))
    n_code_bits = max(1, (max(codes)).bit_length())
    for cb in reversed(range(n_code_bits)):
        planes = []
        for s in range(nset):
            pref = ((codes[s] >> cb) & 1) == 0
            planes.append((one if s == 0 else full) if pref else zero)
        step(planes)
    for cm in CHUNK_MASKS:
        word = jnp.full((rows, LANES), np.uint32(cm).astype(np.int32), i32)
        step([one] + [word] * n_groups)
    lane = lax.broadcasted_iota(i32, (rows, LANES), 1)
    for lb in reversed(range(7)):
        low = ((lane >> lb) & 1) == 0
        lw = jnp.where(low, full, zero)
        step([lw & 1] + [lw] * n_groups)
    for s in range(nset):
        sel_scr[s] = sel_scr[s] | alive_scr[s]


CHUNK_MASKS = (0x0000FFFF, 0x00FF00FF, 0x0F0F0F0F, 0x33333333, 0x55555555)


def _mask_from_bits(word, bit):
    sh = jnp.full(word.shape, bit, i32)
    picked = (lax.shift_right_logical(word, sh) & 1) != 0
    return jnp.where(picked, 0.0, MASKED_LOGIT).astype(f32)


N_KG = TQ // SUBLANES
META_KG = N_META // SUBLANES
META_TILE = LANES
ONES_ROWS = BF16_ROWS
TILE_BUCKETS = (8, 16, 24, 32)
TILE_SLOTS = 2


class _RefList:
    def __init__(self, refs):
        self.refs = list(refs)

    def _split(self, idx):
        idx = idx if isinstance(idx, tuple) else (idx,)
        return self.refs[idx[0]], (idx[1:] if len(idx) > 1 else (Ellipsis,))

    def __getitem__(self, idx):
        ref, rest = self._split(idx)
        return ref[rest]

    def __setitem__(self, idx, val):
        ref, rest = self._split(idx)
        ref[rest] = val


def _all_sublanes(x, op):
    return jnp.broadcast_to(op(x, axis=0, keepdims=True), x.shape)


def _select_prompt(planes_scr, keym_scr, alive_scr, sel_scr, alivem_scr, selm_scr, krem_scr,
                   n_tiles, i, k_row):
    tile_id = lax.broadcasted_iota(i32, (n_tiles, SUBLANES, TQ), 0)
    zero_t = jnp.zeros((n_tiles, SUBLANES, TQ), i32)
    full_t = jnp.full((n_tiles, SUBLANES, TQ), -1, i32)
    zero_m = jnp.zeros((META_KG, SUBLANES, TQ), i32)
    one_m = jnp.full((META_KG, SUBLANES, TQ), 1, i32)
    alive_scr[0:n_tiles] = jnp.where(tile_id <= i, full_t, zero_t)
    sel_scr[0:n_tiles] = zero_t
    alivem_scr[...] = one_m
    selm_scr[...] = zero_m
    krem_scr[...] = k_row

    def step(pm, pt):
        am = alivem_scr[...]
        at = alive_scr[0:n_tiles]
        om = am & pm
        ot = at & pt
        cnt = jnp.sum(lax.population_count(ot), axis=0) + jnp.sum(om, axis=0)
        tot = _all_sublanes(cnt.astype(f32), jnp.sum)
        krem = krem_scr[...]
        take = tot >= krem
        alive_scr[0:n_tiles] = jnp.where(take[None], ot, at ^ ot)
        sel_scr[0:n_tiles] = sel_scr[0:n_tiles] | jnp.where(take[None], zero_t, ot)
        alivem_scr[...] = jnp.where(take[None], om, am ^ om)
        selm_scr[...] = selm_scr[...] | jnp.where(take[None], zero_m, om)
        krem_scr[...] = jnp.where(take, krem, krem - tot)

    def value_step(t, carry):
        b = WORD_BITS - 1 - t
        sh = jnp.full((META_KG, SUBLANES, TQ), b, i32)
        step(lax.shift_right_logical(keym_scr[...], sh) & 1, planes_scr[0:n_tiles, b])
        return carry

    lax.fori_loop(0, WORD_BITS, value_step, 0)

    step(one_m, zero_t)
    for tb in reversed(range((n_tiles - 1).bit_length())):
        step(one_m, jnp.where(((tile_id >> tb) & 1) == 0, full_t, zero_t))
    for cm in CHUNK_MASKS:
        step(one_m, jnp.full((n_tiles, SUBLANES, TQ), np.uint32(cm).astype(np.int32), i32))
    kg_m = lax.broadcasted_iota(i32, (META_KG, SUBLANES, TQ), 0)
    for gb in reversed(range((META_KG - 1).bit_length())):
        step(jnp.where(((kg_m >> gb) & 1) == 0, one_m, zero_m), full_t)
    sub_t = lax.broadcasted_iota(i32, (n_tiles, SUBLANES, TQ), 1)
    sub_m = lax.broadcasted_iota(i32, (META_KG, SUBLANES, TQ), 1)
    for sb in reversed(range(3)):
        step(jnp.where(((sub_m >> sb) & 1) == 0, one_m, zero_m),
             jnp.where(((sub_t >> sb) & 1) == 0, full_t, zero_t))
    sel_scr[0:n_tiles] = sel_scr[0:n_tiles] | alive_scr[0:n_tiles]
    selm_scr[...] = selm_scr[...] | alivem_scr[...]


def _attn_prompt_body(qit_ref, wit_ref, qt_ref, ki_ref, k_ref, vt_ref, kim_ref, km_ref, vmt_ref,
                      bpp_ref, bmeta_ref, o_ref,
                      planes_scr, keym_scr, alive_scr, sel_scr, alivem_scr, selm_scr, krem_scr,
                      keyw_scr, qm_scr, madd_scr, ot_scr, *per_head, n_tiles, topk):
    groups = [_RefList(per_head[N_HEADS * g:N_HEADS * (g + 1)]) for g in range(3 + 3 * TILE_SLOTS)]
    m_scr, l_scr, acc_scr = groups[0:3]
    s_scr, p_scr, alpha_scr = (groups[3 + g * TILE_SLOTS:3 + (g + 1) * TILE_SLOTS] for g in range(3))
    i = pl.program_id(1)
    halves = [slice(LANES * qh, LANES * (qh + 1)) for qh in range(TQ // LANES)]
    zhalf = jnp.zeros((HEAD_DIM, TQ), bf16)

    qim = [jnp.concatenate([qit_ref[0, IDX_DIM * h:IDX_DIM * (h + 1), :], zhalf], axis=0)
           for h in range(N_IDX_HEADS)]
    wrow = [wit_ref[0, h:h + 1, :] for h in range(N_IDX_HEADS)]

    def score_block(kt2):
        acc = None
        for h in range(N_IDX_HEADS):
            s = jnp.dot(kt2, qim[h], preferred_element_type=f32)
            t = jnp.maximum(s, 0.0) * wrow[h]
            acc = t if acc is None else acc + t
        return acc

    def store_planes(t, buf):
        for cols in halves:
            planes = _bit_planes([keyw_scr[buf, SUBLANES * kg:SUBLANES * (kg + 1), cols]
                                  for kg in range(N_KG)])
            for b in range(WORD_BITS):
                planes_scr[t, b, :, cols] = planes[b]

    def key_rows(t):
        return pl.ds(pl.multiple_of(t * TQ, TQ), TQ)

    def tile_keys(t):
        return _score_key(score_block(ki_ref[0, key_rows(t), :]))

    def score_pair(tt, carry):
        keyw_scr[0] = tile_keys(2 * tt)
        keyw_scr[1] = tile_keys(2 * tt + 1)
        store_planes(2 * tt, 0)
        store_planes(2 * tt + 1, 1)
        return carry

    lax.fori_loop(0, i // 2, score_pair, 0)

    @pl.when(i % 2 == 1)
    def _():
        keyw_scr[0] = tile_keys(i - 1)
        store_planes(i - 1, 0)

    krow = lax.broadcasted_iota(i32, (TQ, TQ), 0)
    qcol = lax.broadcasted_iota(i32, (TQ, TQ), 1)
    keyw_scr[1] = jnp.where(krow <= qcol, tile_keys(i), INT_MIN)
    store_planes(i, 1)
    keym_scr[...] = (_score_key(score_block(kim_ref[0:N_META, :])) ^ jnp.int32(INT_MIN)
                     ).reshape(META_KG, SUBLANES, TQ)

    qpos = lax.broadcasted_iota(i32, (SUBLANES, TQ), 1)
    k_row = jnp.minimum(i * TQ + qpos + (N_META + 1), topk).astype(f32)
    lo = 0
    for nb in TILE_BUCKETS:
        nb = min(nb, n_tiles)
        if nb <= lo:
            continue

        @pl.when((i >= lo) & (i < nb))
        def _(nb=nb):
            _select_prompt(planes_scr, keym_scr, alive_scr, sel_scr, alivem_scr, selm_scr, krem_scr,
                           nb, i, k_row)
        lo = nb
    assert lo == n_tiles

    for h in range(N_HEADS):
        qh = qt_ref[0, HEAD_DIM * h:HEAD_DIM * (h + 1), :]
        qm_scr[h] = jnp.concatenate([qh, zhalf] if h % 2 == 0 else [zhalf, qh], axis=0)
    for h in range(N_HEADS):
        m_scr[h] = jnp.full((SUBLANES, TQ), -jnp.inf, f32)
        l_scr[h] = jnp.zeros((SUBLANES, TQ), f32)
        acc_scr[h] = jnp.zeros((HEAD_DIM, TQ), f32)

    def attend(width, jobs):
        n_kg = width // SUBLANES
        ones_rows = jnp.ones((ONES_ROWS, width), bf16)
        for slot, k_pair, _, _ in jobs:
            for h in range(N_HEADS):
                s_scr[slot][h, 0:width] = jnp.dot(k_pair(h // 2), qm_scr[h], preferred_element_type=f32)
        for slot, _, _, bias_fn in jobs:
            for h in range(N_HEADS):
                for cols in halves:
                    vs = []
                    for kg in range(n_kg):
                        rows = slice(SUBLANES * kg, SUBLANES * (kg + 1))
                        v = s_scr[slot][h, rows, cols] + madd_scr[slot, rows, cols]
                        if bias_fn is not None:
                            v = v + bias_fn(h, rows, cols)
                        vs.append(v)
                    mx = vs[0]
                    for v in vs[1:]:
                        mx = jnp.maximum(mx, v)
                    m_old = m_scr[h, :, cols]
                    m_new = jnp.maximum(m_old, _all_sublanes(mx, jnp.max))
                    m_scr[h, :, cols] = m_new
                    alpha_scr[slot][h, :, cols] = jnp.exp2(m_old - m_new)
                    for kk in range(n_kg // 2):
                        pp = jnp.concatenate([jnp.exp2(vs[2 * kk] - m_new),
                                              jnp.exp2(vs[2 * kk + 1] - m_new)], axis=0)
                        p_scr[slot][h, BF16_ROWS * kk:BF16_ROWS * (kk + 1), cols] = pp.astype(bf16)
        for slot, _, vt_head, _ in jobs:
            for h in range(N_HEADS):
                vaug = jnp.concatenate([vt_head(h), ones_rows], axis=0)
                ol = jnp.dot(vaug, p_scr[slot][h, 0:width], preferred_element_type=f32)
                a = alpha_scr[slot][h]
                for rg in range(HEAD_DIM // SUBLANES):
                    rows = slice(SUBLANES * rg, SUBLANES * (rg + 1))
                    acc_scr[h, rows] = a * acc_scr[h, rows] + ol[rows]
                l_scr[h] = a * l_scr[h] + ol[HEAD_DIM:HEAD_DIM + SUBLANES]

    def head_rows(h):
        return slice(HEAD_DIM * h, HEAD_DIM * (h + 1))

    def pair_cols(hp):
        return slice(PAIR * hp, PAIR * (hp + 1))

    for kg in range(META_TILE // SUBLANES):
        rows = slice(SUBLANES * kg, SUBLANES * (kg + 1))
        if kg < META_KG:
            madd_scr[0, rows] = _mask_from_bits(selm_scr[kg], 0)
        else:
            madd_scr[0, rows] = jnp.full((SUBLANES, TQ), MASKED_LOGIT, f32)
    meta_k = lambda hp: km_ref[:, pair_cols(hp)]
    meta_v = lambda h: vmt_ref[head_rows(h), :]

    @pl.when(i == 0)
    def _():
        attend(META_TILE, [(0, meta_k, meta_v, lambda h, rows, cols: bmeta_ref[h, rows, cols])])

    @pl.when(i > 0)
    def _():
        attend(META_TILE, [(0, meta_k, meta_v, None)])

    def tile_job(t, slot, bias_fn):
        word = sel_scr[t]
        for kg in range(N_KG):
            madd_scr[slot, SUBLANES * kg:SUBLANES * (kg + 1)] = _mask_from_bits(word, kg)
        return (slot, lambda hp: k_ref[0, key_rows(t), pair_cols(hp)],
                lambda h: vt_ref[0, head_rows(h), key_rows(t)], bias_fn)

    n_far = i - 1

    def far_pair(tt, carry):
        attend(TQ, [tile_job(2 * tt, 0, None), tile_job(2 * tt + 1, 1, None)])
        return carry

    lax.fori_loop(0, n_far // 2, far_pair, 0)

    @pl.when((n_far > 0) & (n_far % 2 == 1))
    def _():
        attend(TQ, [tile_job(n_far - 1, 0, None)])

    near_bias = lambda dd: (lambda h, rows, cols: bpp_ref[dd, h, rows, cols])

    @pl.when(i > 0)
    def _():
        attend(TQ, [tile_job(i - 1, 0, near_bias(1)), tile_job(i, 1, near_bias(0))])

    @pl.when(i == 0)
    def _():
        attend(TQ, [tile_job(0, 1, near_bias(0))])

    for h in range(N_HEADS):
        inv = 1.0 / l_scr[h]
        for rg in range(HEAD_DIM // SUBLANES):
            rows = slice(SUBLANES * rg, SUBLANES * (rg + 1))
            ot_scr[HEAD_DIM * h + SUBLANES * rg:HEAD_DIM * h + SUBLANES * (rg + 1)] = acc_scr[h, rows] * inv
    o_ref[0] = ot_scr[...].T


def _attn_prompt(qit, wit, qt, kib, kb, vtb, kim, km, vmt, bias_pp, bias_meta, topk):
    bsz, seq, _ = kb.shape
    assert seq % TQ == 0
    n_tiles = seq // TQ
    assert n_tiles <= TILE_BUCKETS[-1]
    const = lambda nd: (lambda b, i: (0,) * nd)
    body = functools.partial(_attn_prompt_body, n_tiles=n_tiles, topk=topk)
    one = pl.Buffered(1)
    per_head = lambda shape, dt: [pltpu.VMEM(shape, dt) for _ in range(N_HEADS)]
    return pl.pallas_call(
        body,
        grid=(bsz, n_tiles),
        in_specs=[
            pl.BlockSpec((1, N_IDX_HEADS * IDX_DIM, TQ), lambda b, i: (b, 0, i)),
            pl.BlockSpec((1, SUBLANES, TQ), lambda b, i: (b, 0, i)),
            pl.BlockSpec((1, ATTN_WIDTH, TQ), lambda b, i: (b, 0, i)),
            pl.BlockSpec((1, seq, LANES), lambda b, i: (b, 0, 0), pipeline_mode=one),
            pl.BlockSpec((1, seq, ATTN_WIDTH), lambda b, i: (b, 0, 0), pipeline_mode=one),
            pl.BlockSpec((1, ATTN_WIDTH, seq), lambda b, i: (b, 0, 0), pipeline_mode=one),
            pl.BlockSpec((META_TILE, LANES), const(2)),
            pl.BlockSpec((META_TILE, ATTN_WIDTH), const(2)),
            pl.BlockSpec((ATTN_WIDTH, META_TILE), const(2)),
            pl.BlockSpec((2, N_HEADS, TQ, TQ), const(4), pipeline_mode=one),
            pl.BlockSpec((N_HEADS, META_TILE, TQ), const(3), pipeline_mode=one),
        ],
        out_specs=pl.BlockSpec((1, TQ, ATTN_WIDTH), lambda b, i: (b, i, 0)),
        out_shape=jax.ShapeDtypeStruct((bsz, seq, ATTN_WIDTH), f32),
        scratch_shapes=[
            pltpu.VMEM((n_tiles, WORD_BITS, SUBLANES, TQ), i32),
            pltpu.VMEM((META_KG, SUBLANES, TQ), i32),
            pltpu.VMEM((n_tiles, SUBLANES, TQ), i32),
            pltpu.VMEM((n_tiles, SUBLANES, TQ), i32),
            pltpu.VMEM((META_KG, SUBLANES, TQ), i32),
            pltpu.VMEM((META_KG, SUBLANES, TQ), i32),
            pltpu.VMEM((SUBLANES, TQ), f32),
            pltpu.VMEM((2, TQ, TQ), i32),
            pltpu.VMEM((N_HEADS, PAIR, TQ), bf16),
            pltpu.VMEM((TILE_SLOTS, TQ, TQ), f32),
            pltpu.VMEM((ATTN_WIDTH, TQ), f32),
        ] + per_head((SUBLANES, TQ), f32)
          + per_head((SUBLANES, TQ), f32)
          + per_head((HEAD_DIM, TQ), f32)
          + per_head((TQ, TQ), f32) * TILE_SLOTS
          + per_head((TQ, TQ), bf16) * TILE_SLOTS
          + per_head((SUBLANES, TQ), f32) * TILE_SLOTS,
        compiler_params=_cparams(2),
    )(qit, wit, qt, kib, kb, vtb, kim, km, vmt, bias_pp, bias_meta)


PAGES_PER_STEP = 32
IDX_PAGES_PER_STEP = 64
SROWS = 8
SAMPLE_SUB = 16


def _sample_scores_body(pt_ref, qi_ref, wb_ref, kin_ref, *rest):
    page_refs = rest[:IDX_PAGES_PER_STEP]
    out_ref, outn_ref = rest[IDX_PAGES_PER_STEP:]
    qi = qi_ref[0]
    wb = wb_ref[0]

    def combine(s):
        t = jnp.maximum(s, 0.0) * wb
        acc = t[0:SROWS]
        for h in range(1, N_IDX_HEADS):
            acc = acc + t[SROWS * h:SROWS * (h + 1)]
        return acc

    zpad = jnp.zeros((LANES - IDX_DIM, LANES), bf16)
    for jx in range(IDX_PAGES_PER_STEP):
        kt = jnp.concatenate([page_refs[jx][0].astype(bf16), zpad], axis=0)
        out_ref[0, :, LANES * jx:LANES * (jx + 1)] = combine(
            jnp.dot(qi, kt, preferred_element_type=f32))
    outn_ref[0] = combine(lax.dot_general(qi, kin_ref[0], NT_DIMS, preferred_element_type=f32))


def _sample_scores(page_table, qi32, wb32, kin2, kidx_t):
    dbsz, n_pages = page_table.shape
    page = kidx_t.shape[2]
    assert page == LANES and n_pages % IDX_PAGES_PER_STEP == 0
    nsteps = n_pages // IDX_PAGES_PER_STEP
    mrows = N_IDX_HEADS * SROWS

    def page_spec(jx):
        return pl.BlockSpec((1, IDX_DIM, page),
                            lambda b, s, pt: (pt[b, s * IDX_PAGES_PER_STEP + jx], 0, 0))

    grid_spec = pltpu.PrefetchScalarGridSpec(
        num_scalar_prefetch=1,
        grid=(dbsz, nsteps),
        in_specs=[pl.BlockSpec((1, mrows, LANES), lambda b, s, pt: (b, 0, 0)),
                  pl.BlockSpec((1, mrows, LANES), lambda b, s, pt: (b, 0, 0)),
                  pl.BlockSpec((1, LANES, LANES), lambda b, s, pt: (b, 0, 0))]
                 + [page_spec(jx) for jx in range(IDX_PAGES_PER_STEP)],
        out_specs=[pl.BlockSpec((1, SROWS, IDX_PAGES_PER_STEP * LANES), lambda b, s, pt: (b, 0, s)),
                   pl.BlockSpec((1, SROWS, LANES), lambda b, s, pt: (b, 0, 0))],
    )
    return pl.pallas_call(
        _sample_scores_body,
        grid_spec=grid_spec,
        out_shape=[jax.ShapeDtypeStruct((dbsz, SROWS, n_pages * page), f32),
                   jax.ShapeDtypeStruct((dbsz, SROWS, LANES), f32)],
        compiler_params=_cparams(2),
    )(page_table, qi32, wb32, kin2, *([kidx_t] * IDX_PAGES_PER_STEP))


SEL_ROWS = 128


def _sample_select_body(sp_ref, sn_ref, mp_ref, mn_ref, planes_scr, alive_scr, sel_scr, krem_scr,
                        *, n_groups, n_new, topk):
    n_chunks = sp_ref.shape[1] // LANES

    def key_chunk(c, carry):
        col = pl.ds(pl.multiple_of(c * LANES, LANES), LANES)
        planes_scr[c // WORD_BITS, c % WORD_BITS] = _score_key(sp_ref[:, col])
        return carry

    lax.fori_loop(0, n_chunks, key_chunk, 0)
    for c in range(n_chunks, n_groups * WORD_BITS):
        planes_scr[c // WORD_BITS, c % WORD_BITS] = jnp.full((SEL_ROWS, LANES), INT_MIN, i32)
    lane = lax.broadcasted_iota(i32, (SEL_ROWS, LANES), 1)
    row = lax.broadcasted_iota(i32, (SEL_ROWS, LANES), 0)
    causal_new = (lane <= (row & (SROWS - 1))) & (lane < n_new)
    key_n = jnp.where(causal_new, _score_key(sn_ref[...]), INT_MIN) ^ jnp.int32(INT_MIN)

    for g in range(n_groups):
        _transpose_group(planes_scr, g, SEL_ROWS)
    full = jnp.full((SEL_ROWS, LANES), -1, i32)
    alive_init = [jnp.full((SEL_ROWS, LANES), 1, i32)] + [full] * n_groups
    k_row = jnp.full((SEL_ROWS, LANES), topk, f32)
    _radix_select(planes_scr, key_n, alive_scr, sel_scr, krem_scr, n_groups, False, alive_init, k_row)

    def mask_chunk(c, carry):
        col = pl.ds(pl.multiple_of(c * LANES, LANES), LANES)
        mp_ref[:, col] = _mask_from_bits(sel_scr[1 + c // WORD_BITS], c % WORD_BITS)
        return carry

    lax.fori_loop(0, n_chunks, mask_chunk, 0)
    mn_ref[...] = _mask_from_bits(sel_scr[0], 0)


def _sample_select(scores_past, scores_new, n_new, topk):
    rows, past = scores_past.shape
    assert rows % SEL_ROWS == 0 and past % LANES == 0
    n_groups = -(-past // GROUP_KEYS)
    body = functools.partial(_sample_select_body, n_groups=n_groups, n_new=n_new, topk=topk)
    return pl.pallas_call(
        body,
        grid=(rows // SEL_ROWS,),
        in_specs=[pl.BlockSpec((SEL_ROWS, past), lambda r: (r, 0)),
                  pl.BlockSpec((SEL_ROWS, LANES), lambda r: (r, 0))],
        out_specs=[pl.BlockSpec((SEL_ROWS, past), lambda r: (r, 0)),
                   pl.BlockSpec((SEL_ROWS, LANES), lambda r: (r, 0))],
        out_shape=[jax.ShapeDtypeStruct((rows, past), f32),
                   jax.ShapeDtypeStruct((rows, LANES), f32)],
        scratch_shapes=[
            pltpu.VMEM((n_groups, WORD_BITS, SEL_ROWS, LANES), i32),
            pltpu.VMEM((n_groups + 1, SEL_ROWS, LANES), i32),
            pltpu.VMEM((n_groups + 1, SEL_ROWS, LANES), i32),
            pltpu.VMEM((SEL_ROWS, LANES), f32),
        ],
        compiler_params=_cparams(1),
    )(scores_past, scores_new)


def _sample_attend_body(pt_ref, q_ref, mp_ref, mn_ref, knt_ref, vnt_ref, bs_ref, bn_ref, *rest):
    k_refs = rest[:PAGES_PER_STEP]
    v_refs = rest[PAGES_PER_STEP:2 * PAGES_PER_STEP]
    o_ref, m_scr, l_scr, acc_scr, kcat_scr, vcat_scr, s_scr, p_scr, alpha_scr = rest[2 * PAGES_PER_STEP:]
    s_idx = pl.program_id(1)
    last_step = s_idx == pl.num_programs(1) - 1
    mrows = N_HEADS * SROWS
    q = q_ref[0]

    @pl.when(s_idx == 0)
    def _():
        m_scr[...] = jnp.full((mrows, LANES), -jnp.inf, f32)
        l_scr[...] = jnp.zeros((mrows, LANES), f32)
        acc_scr[...] = jnp.zeros((mrows, ATTN_WIDTH), f32)

    def attend(width, bias_tail, maskadd):
        s_scr[:, 0:width] = jnp.dot(q, kcat_scr[:, 0:width], preferred_element_type=f32)
        mk = jnp.concatenate([maskadd] * (SAMPLE_SUB // SROWS), axis=0)
        for r0 in range(0, mrows, SAMPLE_SUB):
            rows = slice(r0, r0 + SAMPLE_SUB)
            s = s_scr[rows, 0:width] + mk
            tail = s[:, width - LANES:] + bias_tail[rows]
            s = tail if width == LANES else jnp.concatenate([s[:, :width - LANES], tail], axis=1)
            m_old = m_scr[rows]
            m_new = jnp.maximum(m_old, jnp.max(s, axis=-1, keepdims=True))
            alpha = jnp.exp2(m_old - m_new)
            m_scr[rows] = m_new
            m_rep = m_new if width == LANES else jnp.concatenate([m_new] * (width // LANES), axis=1)
            p = jnp.exp2(s - m_rep)
            l_scr[rows] = alpha * l_scr[rows] + jnp.sum(p, axis=-1, keepdims=True)
            p_scr[rows, 0:width] = p.astype(bf16)
            alpha_scr[rows] = alpha
        o = lax.dot_general(p_scr[:, 0:width], vcat_scr[:, 0:width], NT_DIMS, preferred_element_type=f32)
        aw = jnp.concatenate([alpha_scr[...]] * (ATTN_WIDTH // LANES), axis=1)
        acc_scr[...] = aw * acc_scr[...] + o

    for jx in range(PAGES_PER_STEP):
        kcat_scr[:, LANES * jx:LANES * (jx + 1)] = k_refs[jx][0].astype(bf16)
        vcat_scr[:, LANES * jx:LANES * (jx + 1)] = v_refs[jx][0].astype(bf16)
    attend(PAGES_PER_STEP * LANES, bs_ref[...] * jnp.where(last_step, 1.0, 0.0), mp_ref[0])

    @pl.when(last_step)
    def _():
        kcat_scr[:, 0:LANES] = knt_ref[0]
        vcat_scr[:, 0:LANES] = vnt_ref[0]
        attend(LANES, bn_ref[...], mn_ref[0])
        l_w = jnp.concatenate([l_scr[...]] * (ATTN_WIDTH // LANES), axis=1)
        accn = acc_scr[...] / l_w
        head_of_lane = lax.broadcasted_iota(i32, (SROWS, ATTN_WIDTH), 1) // HEAD_DIM
        out = jnp.zeros((SROWS, ATTN_WIDTH), f32)
        for h in range(N_HEADS):
            out = out + jnp.where(head_of_lane == h, accn[SROWS * h:SROWS * (h + 1)], 0.0)
        o_ref[0] = out


def _sample_attend(page_table, q64, mask_past, mask_new, knt, vnt, bias_s, bias_n, k_t, v_t):
    dbsz, n_pages = page_table.shape
    page = k_t.shape[2]
    assert page == LANES and n_pages % PAGES_PER_STEP == 0
    nsteps = n_pages // PAGES_PER_STEP
    mrows = N_HEADS * SROWS
    width = PAGES_PER_STEP * LANES

    def page_spec(jx):
        return pl.BlockSpec((1, ATTN_WIDTH, page), lambda b, s, pt: (pt[b, s * PAGES_PER_STEP + jx], 0, 0))

    grid_spec = pltpu.PrefetchScalarGridSpec(
        num_scalar_prefetch=1,
        grid=(dbsz, nsteps),
        in_specs=[pl.BlockSpec((1, mrows, ATTN_WIDTH), lambda b, s, pt: (b, 0, 0)),
                  pl.BlockSpec((1, SROWS, width), lambda b, s, pt: (b, 0, s)),
                  pl.BlockSpec((1, SROWS, LANES), lambda b, s, pt: (b, 0, 0)),
                  pl.BlockSpec((1, ATTN_WIDTH, LANES), lambda b, s, pt: (b, 0, 0)),
                  pl.BlockSpec((1, ATTN_WIDTH, LANES), lambda b, s, pt: (b, 0, 0)),
                  pl.BlockSpec((mrows, LANES), lambda b, s, pt: (0, 0)),
                  pl.BlockSpec((mrows, LANES), lambda b, s, pt: (0, 0))]
                 + [page_spec(jx) for jx in range(PAGES_PER_STEP)]
                 + [page_spec(jx) for jx in range(PAGES_PER_STEP)],
        out_specs=pl.BlockSpec((1, SROWS, ATTN_WIDTH), lambda b, s, pt: (b, 0, 0)),
        scratch_shapes=[pltpu.VMEM((mrows, LANES), f32),
                        pltpu.VMEM((mrows, LANES), f32),
                        pltpu.VMEM((mrows, ATTN_WIDTH), f32),
                        pltpu.VMEM((ATTN_WIDTH, width), bf16),
                        pltpu.VMEM((ATTN_WIDTH, width), bf16),
                        pltpu.VMEM((mrows, width), f32),
                        pltpu.VMEM((mrows, width), bf16),
                        pltpu.VMEM((mrows, LANES), f32)],
    )
    return pl.pallas_call(
        _sample_attend_body,
        grid_spec=grid_spec,
        out_shape=jax.ShapeDtypeStruct((dbsz, SROWS, ATTN_WIDTH), f32),
        compiler_params=_cparams(2),
    )(page_table, q64, mask_past, mask_new, knt, vnt, bias_s, bias_n,
      *([k_t] * PAGES_PER_STEP), *([v_t] * PAGES_PER_STEP))


def _silu(x):
    return x * jax.nn.sigmoid(x)


def _merge_core(x, d, gp, ao, ga, gtp, gta, pw_ref, ps_ref, wbp_ref, wba_ref, wo_ref, lng_ref, lnb_ref):
    gw = POOL_WIDTH // len(POOL_WINDOWS)
    parts = [jnp.dot(d[:, gw * g:gw * (g + 1)].astype(bf16), pw_ref[g], preferred_element_type=f32)
             for g in range(len(POOL_WINDOWS))]
    pool_o = jnp.concatenate(parts, axis=1) * ps_ref[...]
    bp = jnp.dot((pool_o * _silu(gp)).astype(bf16), wbp_ref[...], preferred_element_type=f32)
    ba = jnp.dot((ao * _silu(ga)).astype(bf16), wba_ref[...], preferred_element_type=f32)
    m = jax.nn.sigmoid(gtp) * bp + jax.nn.sigmoid(gta) * ba
    out = jnp.dot(m.astype(bf16), wo_ref[...], preferred_element_type=f32)
    z = ALPHA * x + out
    mu = jnp.mean(z, axis=-1, keepdims=True)
    zc = z - mu
    var = jnp.mean(zc * zc, axis=-1, keepdims=True)
    return zc * lax.rsqrt(var + LN_EPS) * lng_ref[...] + lnb_ref[...]


def _merge_prompt_body(x_ref, u_ref, uh_ref, um_ref, gp_ref, ao_ref, ga_ref, gtp_ref, gta_ref,
                       pw_ref, ps_ref, wbp_ref, wba_ref, wo_ref, lng_ref, lnb_ref, y_ref, uu_scr):
    t = pl.program_id(1)
    halo = N_META
    uu_scr[0:halo, :] = jnp.where(t == 0, um_ref[...], uh_ref[0])
    uu_scr[halo:halo + TQ, :] = u_ref[0]
    gw = POOL_WIDTH // len(POOL_WINDOWS)
    parts = []
    for g, w in enumerate(POOL_WINDOWS):
        cols = slice(gw * g, gw * (g + 1))
        s = uu_scr[halo:halo + TQ, cols]
        for j in range(1, w):
            s = s + uu_scr[halo - j:halo - j + TQ, cols]
        parts.append(s * (1.0 / w) - uu_scr[halo:halo + TQ, cols])
    d = jnp.concatenate(parts, axis=1)
    y_ref[0] = _merge_core(x_ref[0], d, gp_ref[0], ao_ref[0], ga_ref[0], gtp_ref[0], gta_ref[0],
                           pw_ref, ps_ref, wbp_ref, wba_ref, wo_ref, lng_ref, lnb_ref)


def _weight_specs(nd_grid):
    z = lambda nd: (lambda *a: (0,) * nd)
    gw = POOL_WIDTH // len(POOL_WINDOWS)
    return [pl.BlockSpec((len(POOL_WINDOWS), gw, gw), z(3)),
            pl.BlockSpec((1, POOL_WIDTH), z(2)),
            pl.BlockSpec((POOL_WIDTH, D_MODEL), z(2)),
            pl.BlockSpec((ATTN_WIDTH, D_MODEL), z(2)),
            pl.BlockSpec((D_MODEL, D_MODEL), z(2)),
            pl.BlockSpec((1, D_MODEL), z(2)),
            pl.BlockSpec((1, D_MODEL), z(2))]


def _merge_prompt(x, u, u_meta, gp, ao, ga, gtp, gta, weights):
    bsz, seq, _ = x.shape
    nt = seq // TQ
    hb = TQ // N_META
    tile = lambda n: pl.BlockSpec((1, TQ, n), lambda b, t: (b, t, 0))
    return pl.pallas_call(
        _merge_prompt_body,
        grid=(bsz, nt),
        in_specs=[tile(D_MODEL), tile(POOL_WIDTH),
                  pl.BlockSpec((1, N_META, POOL_WIDTH), lambda b, t: (b, jnp.maximum(t * hb - 1, 0), 0)),
                  pl.BlockSpec((N_META, POOL_WIDTH), lambda b, t: (0, 0)),
                  tile(POOL_WIDTH), tile(ATTN_WIDTH), tile(ATTN_WIDTH), tile(D_MODEL), tile(D_MODEL)]
                 + _weight_specs(2),
        out_specs=tile(D_MODEL),
        out_shape=jax.ShapeDtypeStruct((bsz, seq, D_MODEL), f32),
        scratch_shapes=[pltpu.VMEM((N_META + TQ, POOL_WIDTH), f32)],
        compiler_params=_cparams(2),
    )(x, u, u, u_meta, gp, ao, ga, gtp, gta, *weights)


def _merge_sample_body(x_ref, u_ref, sp_ref, gp_ref, ao_ref, ga_ref, gtp_ref, gta_ref,
                       pw_ref, ps_ref, wbp_ref, wba_ref, wo_ref, lng_ref, lnb_ref, y_ref, *, tnew):
    gw = POOL_WIDTH // len(POOL_WINDOWS)

    def hist(t, cols):
        if t < POOL_BUF:
            return sp_ref[:, POOL_WIDTH * t + cols.start:POOL_WIDTH * t + cols.stop]
        tt = t - POOL_BUF
        return u_ref[:, POOL_WIDTH * tt + cols.start:POOL_WIDTH * tt + cols.stop]

    for i in range(tnew):
        parts = []
        for g, w in enumerate(POOL_WINDOWS):
            cols = slice(gw * g, gw * (g + 1))
            s = hist(POOL_BUF + i, cols)
            for j in range(1, w):
                s = s + hist(POOL_BUF + i - j, cols)
            parts.append(s * (1.0 / w) - hist(POOL_BUF + i, cols))
        d = jnp.concatenate(parts, axis=1)
        c5 = slice(POOL_WIDTH * i, POOL_WIDTH * (i + 1))
        c10 = slice(D_MODEL * i, D_MODEL * (i + 1))
        y_ref[:, c10] = _merge_core(x_ref[:, c10], d, gp_ref[:, c5], ao_ref[:, c5], ga_ref[:, c5],
                                    gtp_ref[:, c10], gta_ref[:, c10],
                                    pw_ref, ps_ref, wbp_ref, wba_ref, wo_ref, lng_ref, lnb_ref)


def _merge_sample(x, u, state_pool, gp, ao, ga, gtp, gta, weights, tnew):
    dbsz = x.shape[0]
    full = lambda a: pl.BlockSpec(a.shape, lambda i: (0, 0))
    args = (x, u, state_pool, gp, ao, ga, gtp, gta)
    body = functools.partial(_merge_sample_body, tnew=tnew)
    return pl.pallas_call(
        body,
        grid=(1,),
        in_specs=[full(a) for a in args] + _weight_specs(1),
        out_specs=pl.BlockSpec((dbsz, tnew * D_MODEL), lambda i: (0, 0)),
        out_shape=jax.ShapeDtypeStruct((dbsz, tnew * D_MODEL), f32),
        compiler_params=_cparams(1),
    )(*args, *weights)


def _rel_bucket(n):
    max_exact = N_BUCKETS // 2
    nf = jnp.maximum(n, 1).astype(f32)
    large = max_exact + (jnp.log(nf / max_exact) / math.log(MAX_DISTANCE / max_exact)
                         * (N_BUCKETS - max_exact)).astype(i32)
    large = jnp.minimum(large, N_BUCKETS - 1)
    return jnp.where(n < max_exact, n, large)


def _bias_table(rel_bias, dist):
    bucket = _rel_bucket(jnp.maximum(dist, 0))
    rb = rel_bias.astype(f32)
    far = rb[N_BUCKETS - 1]
    shape = (N_HEADS,) + (1,) * dist.ndim
    out = jnp.zeros((N_HEADS,) + dist.shape, f32)
    for b in range(N_BUCKETS):
        out = jnp.where(bucket[None] == b, (rb[b] - far).reshape(shape), out)
    return out * LOG2E


def _split_w_in(w_in):
    splits = (512, 512, 512, 512, 512, 512, 256, 64, 4, 1024, 1024)
    offs = np.cumsum((0,) + splits)
    return {n: w_in[:, offs[i]:offs[i + 1]] for i, n in enumerate(
        ("u", "gp", "q", "k", "v", "ga", "qi", "ki", "wi", "gate_p", "gate_a"))}


def _pad_w_in(w_in):
    part = _split_w_in(w_in)
    cols = [part["u"], part["gp"], part["k"], part["ga"], part["ki"], part["ki"],
            part["gate_p"], part["gate_a"]]
    w = jnp.concatenate(cols, axis=1)
    assert w.shape[1] == W_COLS
    zrows = jnp.zeros((w_in.shape[0], BF16_ROWS - N_IDX_HEADS), w_in.dtype)
    wt = jnp.concatenate([part["k"], part["v"], part["q"], part["qi"] * (IDX_DIM ** -0.5), part["ki"],
                          part["wi"] * (N_IDX_HEADS ** -0.5), zrows], axis=1).T
    assert wt.shape[0] == WT_ROWS
    return w.astype(bf16), wt.astype(bf16)


def kernel(x_prompt, x_sample, cache_k, cache_v, cache_kidx, state_pool, page_table, meta, w_in,
           pool_w, pool_scale, w_br_pool, w_br_attn, rel_bias, w_out, ln_g, ln_b):
    bsz, seq, _ = x_prompt.shape
    dbsz, tnew, _ = x_sample.shape
    n_phys, page = cache_k.shape[0], cache_k.shape[1]
    n_pages = page_table.shape[1]
    past_len = n_pages * page
    assert (IDX_DIM ** -0.5, N_IDX_HEADS ** -0.5) == (0.125, 0.5)
    assert tnew <= SROWS and N_META <= LANES and tnew <= POOL_BUF

    w_pad, wt_pad = _pad_w_in(w_in)
    weights = (pool_w.astype(bf16), pool_scale.reshape(1, POOL_WIDTH).astype(f32),
               w_br_pool.astype(bf16), w_br_attn.astype(bf16), w_out.astype(bf16),
               ln_g.reshape(1, D_MODEL).astype(f32), ln_b.reshape(1, D_MODEL).astype(f32))

    (u, gp, kb, ga, kib, gtp, gta, kt, vt, vtb, qt, qit, kidxt, wit) = _project(
        x_prompt.reshape(bsz * seq, D_MODEL), w_pad, wt_pad, TQ, groups=bsz)
    (u_m, _, kb_m, _, kib_m, _, _, kt_m, vt_m, vtb_m, _, _, kidxt_m, _) = _project(
        meta.astype(x_prompt.dtype), w_pad, wt_pad, N_META)
    r3 = lambda a: a.reshape(bsz, seq, a.shape[-1])
    pad_rows = lambda a: jnp.pad(a, ((0, META_TILE - N_META), (0, 0)))
    pad_cols = lambda a: jnp.pad(a, ((0, 0), (0, META_TILE - N_META)))

    ar = jnp.arange(TQ)
    assert TQ >= MAX_DISTANCE
    d_pp = jnp.stack([ar[None, :] - ar[:, None] + TQ * dd for dd in range(2)])
    bias_pp = jnp.swapaxes(_bias_table(rel_bias, d_pp), 0, 1)
    d_m = ar[None, :] + N_META - jnp.arange(META_TILE)[:, None]
    bias_meta = _bias_table(rel_bias, d_m)

    topk_p = min(TOPK_MAX, seq // 4)
    attn_o = _attn_prompt(qit, wit, qt, r3(kib), r3(kb), vtb,
                          pad_rows(kib_m), pad_rows(kb_m), pad_cols(vtb_m[0]), bias_pp, bias_meta, topk_p)
    y_prompt = _merge_prompt(x_prompt, r3(u), u_m, r3(gp), attn_o, r3(ga), r3(gtp), r3(gta), weights)

    def with_meta_t(m_t, a_t):
        m3 = jnp.broadcast_to(m_t, (bsz,) + m_t.shape[1:])
        return jnp.swapaxes(jnp.concatenate([m3, a_t], axis=2), 1, 2)

    k_prompt = with_meta_t(kt_m, kt).reshape(bsz, seq + N_META, N_HEADS, HEAD_DIM)
    v_prompt = with_meta_t(vt_m, vt).reshape(bsz, seq + N_META, N_HEADS, HEAD_DIM)
    kidx_prompt = with_meta_t(kidxt_m, kidxt)
    pool_prompt = r3(u)[:, seq - POOL_BUF:]

    (us, gps, _, gas, kibs, gtps, gtas, kts, vts, _, qts, qits, kidxts, wits) = _project(
        x_sample.reshape(dbsz * tnew, D_MODEL), w_pad, wt_pad, min(TQ, dbsz * tnew))
    s3 = lambda a: a.reshape(dbsz, tnew, a.shape[-1])
    t3 = lambda a_t: jnp.transpose(a_t.reshape(a_t.shape[1], dbsz, tnew), (1, 2, 0))
    padq = lambda a: jnp.pad(a, ((0, 0), (0, 0), (0, SROWS - tnew), (0, 0)))
    qi4 = jnp.swapaxes(t3(qits).reshape(dbsz, tnew, N_IDX_HEADS, IDX_DIM), 1, 2)
    qi32 = jnp.pad(padq(qi4), ((0, 0), (0, 0), (0, 0), (0, LANES - IDX_DIM)))
    qi32 = qi32.reshape(dbsz, N_IDX_HEADS * SROWS, LANES)
    wi4 = jnp.swapaxes(t3(wits)[:, :, :N_IDX_HEADS], 1, 2)[..., None]
    wb32 = jnp.broadcast_to(padq(wi4), (dbsz, N_IDX_HEADS, SROWS, LANES))
    wb32 = wb32.reshape(dbsz, N_IDX_HEADS * SROWS, LANES)
    kin2 = jnp.pad(s3(kibs), ((0, 0), (0, LANES - tnew), (0, 0)))
    kidx_t = jnp.swapaxes(cache_kidx, 1, 2)
    k_t = jnp.transpose(cache_k, (0, 2, 3, 1)).reshape(n_phys, ATTN_WIDTH, page)
    v_t = jnp.transpose(cache_v, (0, 2, 3, 1)).reshape(n_phys, ATTN_WIDTH, page)
    scores_past, scores_new = _sample_scores(page_table, qi32, wb32, kin2, kidx_t)

    topk_s = min(TOPK_MAX, (past_len + tnew) // 4)
    mask_past, mask_new = _sample_select(scores_past.reshape(dbsz * SROWS, past_len),
                                         scores_new.reshape(dbsz * SROWS, LANES), tnew, topk_s)

    head_of_lane = jnp.arange(ATTN_WIDTH) // HEAD_DIM
    q4 = jnp.where(head_of_lane[None, None, None, :] == jnp.arange(N_HEADS)[None, :, None, None],
                   t3(qts)[:, None, :, :], jnp.zeros((), bf16))
    q64 = padq(q4).reshape(dbsz, N_HEADS * SROWS, ATTN_WIDTH)
    new_t = lambda a_t: jnp.pad(jnp.swapaxes(a_t.reshape(ATTN_WIDTH, dbsz, tnew), 0, 1).astype(bf16),
                                ((0, 0), (0, 0), (0, LANES - tnew)))
    qrow = jnp.arange(SROWS)
    assert page >= MAX_DISTANCE
    d_s = (page + qrow)[:, None] - jnp.arange(LANES)[None, :]
    bias_s = _bias_table(rel_bias, d_s).reshape(N_HEADS * SROWS, LANES)
    d_n = qrow[:, None] - jnp.arange(LANES)[None, :]
    bias_n = _bias_table(rel_bias, d_n).reshape(N_HEADS * SROWS, LANES)
    ao_s = _sample_attend(page_table, q64,
                          mask_past.reshape(dbsz, SROWS, past_len), mask_new.reshape(dbsz, SROWS, LANES),
                          new_t(kts), new_t(vts), bias_s, bias_n, k_t, v_t)
    ao_s = ao_s[:, :tnew].reshape(dbsz, tnew * ATTN_WIDTH)

    flat = lambda a: a.reshape(dbsz, -1)
    y_sample = _merge_sample(flat(x_sample), flat(us), flat(state_pool.astype(f32)), flat(gps), ao_s,
                             flat(gas), flat(gtps), flat(gtas), weights, tnew)
    y_sample = y_sample.reshape(dbsz, tnew, D_MODEL)

    k_sample = t3(kts).reshape(dbsz, tnew, N_HEADS, HEAD_DIM)
    v_sample = t3(vts).reshape(dbsz, tnew, N_HEADS, HEAD_DIM)
    kidx_sample = t3(kidxts)
    pool_sample = jnp.concatenate([state_pool.astype(f32), s3(us)], axis=1)[:, -POOL_BUF:]

    return (y_prompt, y_sample, k_prompt, v_prompt, kidx_prompt, pool_prompt,
            k_sample, v_sample, kidx_sample, pool_sample)
```
